```python
import math
import jax, jax.numpy as jnp
from jax import lax
import numpy as np

D_MODEL = 2048
BATCH = 4
SEQ = 2048
DEPTH = 4

N_MIXERS = 4
MIXER_COUNTS = tuple(len(range(m, DEPTH, N_MIXERS)) for m in range(N_MIXERS))
N_DENSE = (DEPTH + 1) // 2
N_MOE = DEPTH // 2

N_HEADS = 16
HEAD_DIM = D_MODEL // N_HEADS
MOBA_BLOCK = 256
MOBA_TOPK = 3
MOBA_QCHUNK = 8
ROPE_THETA = 10000.0
SC_WIDTH = 3
POOL_WINDOWS = (2, 4, 8, 16)
N_POOL_GROUPS = len(POOL_WINDOWS)
POOL_GROUP = D_MODEL // N_POOL_GROUPS
CF_WIDTH = 31
D_FF = 256 * ((8 * D_MODEL // 3 + 255) // 256)
N_EXPERTS = 8
MOE_TOPK = 2
D_FF_EXPERT = D_FF

NORM_EPS = 1e-6
LN_EPS = 1e-5
NEG_INF = -1e30

kernel_name = "hybrid_moba_shortconv_pool_conformer_moe"


def rms_norm(x, g):
    xf = x.astype(jnp.float32)
    y = xf * lax.rsqrt(jnp.mean(xf * xf, axis=-1, keepdims=True) + NORM_EPS)
    return (y * g.astype(jnp.float32)).astype(x.dtype)


def layer_norm(x, g, b):
    xf = x.astype(jnp.float32)
    mu = jnp.mean(xf, axis=-1, keepdims=True)
    var = jnp.mean(jnp.square(xf - mu), axis=-1, keepdims=True)
    y = (xf - mu) * lax.rsqrt(var + LN_EPS)
    return (y * g.astype(jnp.float32) + b.astype(jnp.float32)).astype(x.dtype)


def rope(x, pos):
    half = HEAD_DIM // 2
    inv = ROPE_THETA ** (-jnp.arange(half, dtype=jnp.float32) * (2.0 / HEAD_DIM))
    ang = pos.astype(jnp.float32)[:, None] * inv[None, :]
    cos, sin = jnp.cos(ang), jnp.sin(ang)
    xf = x.astype(jnp.float32)
    x1, x2 = xf[..., :half], xf[..., half:]
    out = jnp.concatenate([x1 * cos - x2 * sin, x2 * cos + x1 * sin], axis=-1)
    return out.astype(x.dtype)


def causal_depthwise_conv(x, w):
    k = w.shape[0]
    return lax.conv_general_dilated(
        x, w[:, None, :].astype(x.dtype), window_strides=(1,), padding=[(k - 1, 0)],
        dimension_numbers=("NWC", "WIO", "NWC"), feature_group_count=x.shape[-1])


def moba_attention(h, w_qkv, w_o):
    b, s, d = h.shape
    L = MOBA_BLOCK
    scale = HEAD_DIM ** -0.5
    q, k, v = jnp.split(h @ w_qkv, 3, axis=-1)
    to_heads = lambda t: t.reshape(b, s, N_HEADS, HEAD_DIM).transpose(0, 2, 1, 3)
    q, k, v = to_heads(q), to_heads(k), to_heads(v)
    pos = jnp.arange(s)
    q = rope(q, pos)
    k = rope(k, pos)

    nb = -(-s // L)
    pad = nb * L - s
    padw = ((0, 0), (0, 0), (0, pad), (0, 0))
    kp = jnp.pad(k, padw).reshape(b, N_HEADS, nb, L, HEAD_DIM)
    vp = jnp.pad(v, padw).reshape(b, N_HEADS, nb, L, HEAD_DIM)
    qp = jnp.pad(q, padw).reshape(b, N_HEADS, nb, L, HEAD_DIM)

    lg = jnp.einsum("bhnqd,bhnkd->bhnqk", qp, kp).astype(jnp.float32) * scale
    causal = jnp.tril(jnp.ones((L, L), dtype=bool))
    lg = jnp.where(causal, lg, NEG_INF)
    lse_own = jax.nn.logsumexp(lg, axis=-1)
    p = jnp.exp(lg - lse_own[..., None])
    o_own = jnp.einsum("bhnqk,bhnkd->bhnqd", p.astype(vp.dtype), vp)
    o_own = o_own.reshape(b, N_HEADS, nb * L, HEAD_DIM)[:, :, :s].astype(jnp.float32)
    lse_own = lse_own.reshape(b, N_HEADS, nb * L)[:, :, :s]

    if nb == 1:
        o = o_own
    else:
        k_sel = min(MOBA_TOPK, nb - 1)
        kbar = jnp.mean(kp.astype(jnp.float32), axis=3)
        gate = jnp.einsum("bhsd,bhnd->bhsn", q.astype(jnp.float32), kbar)
        q_blk = pos // L
        past = jnp.arange(nb)[None, :] < q_blk[:, None]
        gate = jnp.where(past, gate, NEG_INF)
        _, idx = lax.top_k(gate, k_sel)
        valid = idx < q_blk[:, None]

        nc = s // MOBA_QCHUNK

        def chunk(t):
            t = t.reshape(b, N_HEADS, nc, MOBA_QCHUNK, *t.shape[3:])
            return jnp.moveaxis(t, 2, 0)

        def gather_blocks(blocks, ids):
            return jax.vmap(jax.vmap(lambda bl, i: bl[i]))(blocks, ids)

        def attend_chunk(args):
            qc, ic, vc = args
            kg = gather_blocks(kp, ic)
            vg = gather_blocks(vp, ic)
            l = jnp.einsum("bhqd,bhqnkd->bhqnk", qc, kg).astype(jnp.float32) * scale
            l = jnp.where(vc[..., None], l, NEG_INF).reshape(b, N_HEADS, MOBA_QCHUNK, k_sel * L)
            lse = jax.nn.logsumexp(l, axis=-1)
            pr = jnp.exp(l - lse[..., None]).reshape(b, N_HEADS, MOBA_QCHUNK, k_sel, L)
            oc = jnp.einsum("bhqnk,bhqnkd->bhqd", pr.astype(vg.dtype), vg)
            return oc.astype(jnp.float32), lse

        o_past, lse_past = lax.map(attend_chunk, (chunk(q), chunk(idx), chunk(valid)))
        o_past = jnp.moveaxis(o_past, 0, 2).reshape(b, N_HEADS, s, HEAD_DIM)
        lse_past = jnp.moveaxis(lse_past, 0, 2).reshape(b, N_HEADS, s)

        m = jnp.maximum(lse_own, lse_past)
        w_own = jnp.exp(lse_own - m)
        w_past = jnp.exp(lse_past - m)
        o = (w_own[..., None] * o_own + w_past[..., None] * o_past) / (w_own + w_past)[..., None]

    o = o.astype(h.dtype).transpose(0, 2, 1, 3).reshape(b, s, d)
    return o @ w_o


def short_conv_mixer(h, w_in, conv_w, w_out):
    gb, gc, xv = jnp.split(h @ w_in, 3, axis=-1)
    return (gb * causal_depthwise_conv(gc * xv, conv_w)) @ w_out


def pool_mixer(h, w_group, scale):
    b, s, d = h.shape
    hf = h.astype(jnp.float32)
    cs = jnp.pad(lax.cumsum(hf, axis=1), ((0, 0), (1, 0), (0, 0)))
    t = jnp.arange(s)
    outs = []
    for g, w in enumerate(POOL_WINDOWS):
        sl = slice(g * POOL_GROUP, (g + 1) * POOL_GROUP)
        cs_g = cs[..., sl]
        lo = jnp.maximum(t + 1 - w, 0)
        count = (t + 1 - lo).astype(jnp.float32)
        mean = (cs_g[:, 1:] - cs_g[:, lo]) / count[None, :, None]
        outs.append(mean - hf[..., sl])
    pooled = jnp.stack(outs, axis=2).astype(h.dtype)
    mixed = jnp.einsum("bsgc,gce->bsge", pooled, w_group).reshape(b, s, d)
    return mixed * scale


def conformer_conv(h, w1, b1, dw, dw_b, ln_g, ln_b, w2, b2):
    a, g = jnp.split(h @ w1 + b1, 2, axis=-1)
    u = a * jax.nn.sigmoid(g)
    u = causal_depthwise_conv(u, dw) + dw_b
    u = jax.nn.silu(layer_norm(u, ln_g, ln_b))
    return u @ w2 + b2


def swiglu(h, wg, wu, wd):
    return (jax.nn.silu(h @ wg) * (h @ wu)) @ wd


def moe_swiglu(h, router, wg, wu, wd):
    b, s, d = h.shape
    t = h.reshape(b * s, d)
    logits = (t @ router).astype(jnp.float32)
    top_val, top_idx = lax.top_k(logits, MOE_TOPK)
    gates = jax.nn.softmax(top_val, axis=-1)
    comb = jnp.sum(jax.nn.one_hot(top_idx, N_EXPERTS, dtype=jnp.float32) * gates[..., None], axis=1)
    comb = comb.astype(t.dtype)
    out = jnp.zeros_like(t)
    for e in range(N_EXPERTS):
        out = out + comb[:, e:e + 1] * swiglu(t, wg[e], wu[e], wd[e])
    return out.reshape(b, s, d)


def setup_inputs(seed: int = 0) -> dict:
    key = jax.random.key(seed)
    ks = iter(jax.random.split(key, 40))
    nrm = lambda shape, std: jax.random.normal(next(ks), shape, jnp.float32) * std
    d, f, fe = D_MODEL, D_FF, D_FF_EXPERT
    n_a, n_b, n_c, n_d = MIXER_COUNTS
    return {
        "x": nrm((BATCH, SEQ, d), 1.0),
        "c": nrm((BATCH, d), 1.0),
        "mod_w": nrm((DEPTH, d, 6 * d), 0.5 * d ** -0.5),
        "mod_b": nrm((DEPTH, 6 * d), 0.02),
        "norm_mix": 1.0 + nrm((DEPTH, d), 0.05),
        "norm_ffn": 1.0 + nrm((DEPTH, d), 0.05),
        "norm_final": 1.0 + nrm((d,), 0.05),
        "moba_wqkv": nrm((n_a, d, 3 * d), d ** -0.5),
        "moba_wo": nrm((n_a, d, d), d ** -0.5),
        "sc_w_in": nrm((n_b, d, 3 * d), d ** -0.5),
        "sc_conv": nrm((n_b, SC_WIDTH, d), SC_WIDTH ** -0.5),
        "sc_w_out": nrm((n_b, d, d), d ** -0.5),
        "pool_w": nrm((n_c, N_POOL_GROUPS, POOL_GROUP, POOL_GROUP), POOL_GROUP ** -0.5),
        "pool_scale": 1.0 + nrm((n_c, d), 0.1),
        "cf_w1": nrm((n_d, d, 2 * d), d ** -0.5),
        "cf_b1": nrm((n_d, 2 * d), 0.02),
        "cf_dw": nrm((n_d, CF_WIDTH, d), CF_WIDTH ** -0.5),
        "cf_dw_b": nrm((n_d, d), 0.02),
        "cf_ln_g": 1.0 + nrm((n_d, d), 0.05),
        "cf_ln_b": nrm((n_d, d), 0.02),
        "cf_w2": nrm((n_d, d, d), d ** -0.5),
        "cf_b2": nrm((n_d, d), 0.02),
        "ffn_wg": nrm((N_DENSE, d, f), d ** -0.5),
        "ffn_wu": nrm((N_DENSE, d, f), d ** -0.5),
        "ffn_wd": nrm((N_DENSE, f, d), f ** -0.5),
        "moe_router": nrm((N_MOE, d, N_EXPERTS), d ** -0.5),
        "moe_wg": nrm((N_MOE, N_EXPERTS, d, fe), d ** -0.5),
        "moe_wu": nrm((N_MOE, N_EXPERTS, d, fe), d ** -0.5),
        "moe_wd": nrm((N_MOE, N_EXPERTS, fe, d), fe ** -0.5),
    }


def reference(x, c, mod_w, mod_b, norm_mix, norm_ffn, norm_final,
              moba_wqkv, moba_wo, sc_w_in, sc_conv, sc_w_out,
              pool_w, pool_scale,
              cf_w1, cf_b1, cf_dw, cf_dw_b, cf_ln_g, cf_ln_b, cf_w2, cf_b2,
              ffn_wg, ffn_wu, ffn_wd,
              moe_router, moe_wg, moe_wu, moe_wd):
    c_act = jax.nn.silu(c)
    for layer in range(DEPTH):
        mod = c_act @ mod_w[layer] + mod_b[layer]
        sh_a, sc_a, g_a, sh_f, sc_f, g_f = jnp.split(mod[:, None, :], 6, axis=-1)

        h = rms_norm(x, norm_mix[layer]) * (1.0 + sc_a) + sh_a
        m, j = layer % N_MIXERS, layer // N_MIXERS
        if m == 0:
            y = moba_attention(h, moba_wqkv[j], moba_wo[j])
        elif m == 1:
            y = short_conv_mixer(h, sc_w_in[j], sc_conv[j], sc_w_out[j])
        elif m == 2:
            y = pool_mixer(h, pool_w[j], pool_scale[j])
        else:
            y = conformer_conv(h, cf_w1[j], cf_b1[j], cf_dw[j], cf_dw_b[j],
                               cf_ln_g[j], cf_ln_b[j], cf_w2[j], cf_b2[j])
        x = x + g_a * y

        h = rms_norm(x, norm_ffn[layer]) * (1.0 + sc_f) + sh_f
        i = layer // 2
        if layer % 2 == 0:
            y = swiglu(h, ffn_wg[i], ffn_wu[i], ffn_wd[i])
        else:
            y = moe_swiglu(h, moe_router[i], moe_wg[i], moe_wu[i], moe_wd[i])
        x = x + g_f * y
    return rms_norm(x, norm_final)
```

```python
import functools

import jax
import jax.numpy as jnp
from jax import lax
from jax.experimental import pallas as pl
from jax.experimental.pallas import tpu as pltpu

F32 = jnp.float32
BF16 = jnp.bfloat16
I32 = jnp.int32

HEAD_DIM = 128
MOBA_BLOCK = 256
MOBA_TOPK = 3
ROPE_THETA = 10000.0
POOL_WINDOWS = (2, 4, 8, 16)
MOE_TOPK = 2
NORM_EPS = 1e-6
LN_EPS = 1e-5
NEG_INF = -1e30

LANES = 128
VMEM_LIMIT_CAP = 60000 * 1024
MOD_PARTS = 6


def _params(vmem_bytes, n_axes):
    limit = min(int(vmem_bytes) + (6 << 20), VMEM_LIMIT_CAP)
    return pltpu.CompilerParams(
        dimension_semantics=("arbitrary",) * n_axes, vmem_limit_bytes=limit)


def _dot(a, b):
    return jnp.dot(a, b, preferred_element_type=F32)


def _dot_nt(a, b):
    return lax.dot_general(a, b, (((1,), (1,)), ((), ())), preferred_element_type=F32)


def _silu(x):
    return x * jax.nn.sigmoid(x)


def _adaln(x, g, sc, sh):
    ms = jnp.mean(x * x, axis=-1, keepdims=True)
    return (x * lax.rsqrt(ms + NORM_EPS)) * g * (1.0 + sc) + sh


def _mod_kernel(c_ref, w_ref, b_ref, o_ref):
    c = c_ref[...]
    ca = _silu(c).astype(BF16)
    o_ref[...] = _dot(ca, w_ref[...].astype(BF16)) + b_ref[...]


def _modulation(c, mod_w, mod_b):
    depth, d, n = mod_w.shape
    b = c.shape[0]
    tn = 1024
    out = pl.pallas_call(
        _mod_kernel,
        grid=(depth, n // tn),
        in_specs=[
            pl.BlockSpec((b, d), lambda l, j: (0, 0)),
            pl.BlockSpec((None, d, tn), lambda l, j: (l, 0, j)),
            pl.BlockSpec((None, 1, tn), lambda l, j: (l, 0, j)),
        ],
        out_specs=pl.BlockSpec((None, b, tn), lambda l, j: (l, 0, j)),
        out_shape=jax.ShapeDtypeStruct((depth, b, n), F32),
        compiler_params=_params(2 * d * tn * 4 + d * tn * 2, 2),
        name="modulation",
    )(c, mod_w, mod_b.reshape(depth, 1, n))
    return out.reshape(depth * b * MOD_PARTS, 1, d)


class _Mod:
    def __init__(self, table, layer, batch, seq):
        self.table = table
        self.base = layer * batch * MOD_PARTS
        self.seq = seq

    def row(self, part, tok0):
        return self.base + (tok0 // self.seq) * MOD_PARTS + part


def _prenorm_kernel(x_ref, g_ref, sc_ref, sh_ref, o_ref):
    o_ref[...] = _adaln(x_ref[...], g_ref[...], sc_ref[...], sh_ref[...]).astype(o_ref.dtype)


def _prenorm(x, gain, mod, part0, tm=512):
    t, d = x.shape
    row = lambda p: pl.BlockSpec((None, 1, d), lambda i: (mod.row(p, i * tm), 0, 0))
    return pl.pallas_call(
        _prenorm_kernel,
        grid=(t // tm,),
        in_specs=[
            pl.BlockSpec((tm, d), lambda i: (i, 0)),
            pl.BlockSpec((1, d), lambda i: (0, 0)),
            row(part0 + 1), row(part0),
        ],
        out_specs=pl.BlockSpec((tm, d), lambda i: (i, 0)),
        out_shape=jax.ShapeDtypeStruct((t, d), BF16),
        compiler_params=_params(2 * tm * d * 6 + 3 * tm * d * 4, 1),
        name="prenorm",
    )(x, gain.reshape(1, d), mod.table, mod.table)


def _final_norm_kernel(x_ref, g_ref, o_ref):
    x = x_ref[...]
    ms = jnp.mean(x * x, axis=-1, keepdims=True)
    o_ref[...] = x * lax.rsqrt(ms + NORM_EPS) * g_ref[...]


def _final_norm(x, gain, tm=512):
    t, d = x.shape
    return pl.pallas_call(
        _final_norm_kernel,
        grid=(t // tm,),
        in_specs=[pl.BlockSpec((tm, d), lambda i: (i, 0)), pl.BlockSpec((1, d), lambda i: (0, 0))],
        out_specs=pl.BlockSpec((tm, d), lambda i: (i, 0)),
        out_shape=jax.ShapeDtypeStruct((t, d), F32),
        compiler_params=_params(6 * tm * d * 4, 1),
        name="final_norm",
    )(x, gain.reshape(1, d))


def _qkv_kernel(x_ref, w_ref, cos_ref, sin_ref, o_ref, wb_ref, *, nq, scale):
    j = pl.program_id(0)

    @pl.when(pl.program_id(1) == 0)
    def _():
        wb_ref[...] = w_ref[...].astype(BF16)

    acc = _dot(x_ref[...], wb_ref[...])
    tn = acc.shape[1]

    def rope(mult):
        cos = cos_ref[...]
        sin = sin_ref[...]
        for c in range(tn // HEAD_DIM):
            a = acc[:, c * HEAD_DIM:(c + 1) * HEAD_DIM]
            r = a * cos + pltpu.roll(a, HEAD_DIM // 2, 1) * sin
            if mult != 1.0:
                r = r * mult
            o_ref[:, c * HEAD_DIM:(c + 1) * HEAD_DIM] = r.astype(o_ref.dtype)

    @pl.when(j < nq)
    def _():
        rope(scale)

    @pl.when((j >= nq) & (j < 2 * nq))
    def _():
        rope(1.0)

    @pl.when(j >= 2 * nq)
    def _():
        o_ref[...] = acc.astype(o_ref.dtype)


def _qkv_rope(h, w, seq, tm=1024, tn=512):
    t, d = h.shape
    n = w.shape[1]
    half = HEAD_DIM // 2
    inv = ROPE_THETA ** (-jnp.arange(half, dtype=F32) * (2.0 / HEAD_DIM))
    ang = jnp.arange(seq, dtype=F32)[:, None] * inv[None, :]
    cos = jnp.concatenate([jnp.cos(ang), jnp.cos(ang)], axis=-1)
    sin = jnp.concatenate([-jnp.sin(ang), jnp.sin(ang)], axis=-1)
    tm = min(tm, seq)
    sblk = seq // tm
    kern = functools.partial(_qkv_kernel, nq=d // tn, scale=HEAD_DIM ** -0.5)
    return pl.pallas_call(
        kern,
        grid=(n // tn, t // tm),
        in_specs=[
            pl.BlockSpec((tm, d), lambda j, i: (i, 0)),
            pl.BlockSpec((d, tn), lambda j, i: (0, j)),
            pl.BlockSpec((tm, HEAD_DIM), lambda j, i: (i % sblk, 0)),
            pl.BlockSpec((tm, HEAD_DIM), lambda j, i: (i % sblk, 0)),
        ],
        out_specs=pl.BlockSpec((tm, tn), lambda j, i: (i, j)),
        out_shape=jax.ShapeDtypeStruct((t, n), BF16),
        scratch_shapes=[pltpu.VMEM((d, tn), BF16)],
        compiler_params=_params(
            2 * tm * d * 2 + 2 * d * tn * 4 + d * tn * 2 + 2 * tm * tn * 2 + 3 * tm * tn * 4, 2),
        name="qkv_rope",
    )(h, w, cos, sin)


def _moba_kernel(q_ref, k_ref, v_ref, o_ref, kbar_ref, *, nb, blk, topk):
    hd = q_ref.shape[-1]

    kbar_ref[...] = jnp.zeros(kbar_ref.shape, F32)
    for m in range(nb):
        km = k_ref[m * blk:(m + 1) * blk, :].astype(F32)
        kbar_ref[m:m + 1, :] = jnp.mean(km, axis=0, keepdims=True)
    kbar = kbar_ref[...]
    kb_hi = kbar.astype(BF16)
    kb_lo = (kbar - kb_hi.astype(F32)).astype(BF16)

    lane = lax.broadcasted_iota(I32, (blk, LANES), 1)
    row_i = lax.broadcasted_iota(I32, (blk, blk), 0)
    col_i = lax.broadcasted_iota(I32, (blk, blk), 1)

    def qblock(n, carry):
        r0 = pl.multiple_of(n * blk, blk)
        q = q_ref[pl.ds(r0, blk), :]
        gate = _dot_nt(q, kb_hi) + _dot_nt(q, kb_lo)
        valid = lane < n
        g = jnp.where(valid, gate, NEG_INF)
        cnt = jnp.zeros((blk, LANES), F32)
        for mp in range(nb):
            col = g[:, mp:mp + 1]
            beats = jnp.where(col > g, 1.0, jnp.where((col == g) & (lane > mp), 1.0, 0.0))
            cnt = cnt + beats
        sel = jnp.where(valid & (cnt < topk), 1.0, 0.0)

        s = _dot_nt(q, k_ref[pl.ds(r0, blk), :])
        s = jnp.where(row_i >= col_i, s, NEG_INF)
        m0 = jnp.max(s, axis=1, keepdims=True)
        p = jnp.exp(s - m0)
        l0 = jnp.sum(p, axis=1, keepdims=True)
        acc0 = _dot(p.astype(BF16), v_ref[pl.ds(r0, blk), :])

        def past(m, st):
            m_run, l_run, acc = st
            c0 = pl.multiple_of(m * blk, blk)
            selm = jnp.sum(jnp.where(lane == m, sel, 0.0), axis=1, keepdims=True)
            s = _dot_nt(q, k_ref[pl.ds(c0, blk), :])
            s = jnp.where(selm > 0.0, s, NEG_INF)
            m_new = jnp.maximum(m_run, jnp.max(s, axis=1, keepdims=True))
            alpha = jnp.exp(m_run - m_new)
            p = jnp.exp(s - m_new)
            l_new = alpha * l_run + jnp.sum(p, axis=1, keepdims=True)
            acc = alpha * acc + _dot(p.astype(BF16), v_ref[pl.ds(c0, blk), :])
            return m_new, l_new, acc

        _, l_f, acc = lax.fori_loop(0, n, past, (m0, l0, acc0))
        o_ref[pl.ds(r0, blk), :] = (acc / l_f).astype(o_ref.dtype)
        return carry

    lax.fori_loop(0, nb, qblock, 0)


def _moba_attention(qkv, batch, seq, d):
    t = qkv.shape[0]
    heads = d // HEAD_DIM
    blk = MOBA_BLOCK
    assert seq % blk == 0
    nb = seq // blk
    assert nb <= LANES
    kern = functools.partial(_moba_kernel, nb=nb, blk=blk, topk=min(MOBA_TOPK, nb - 1))
    return pl.pallas_call(
        kern,
        grid=(batch, heads),
        in_specs=[
            pl.BlockSpec((seq, HEAD_DIM), lambda b, h: (b, h)),
            pl.BlockSpec((seq, HEAD_DIM), lambda b, h: (b, heads + h)),
            pl.BlockSpec((seq, HEAD_DIM), lambda b, h: (b, 2 * heads + h)),
        ],
        out_specs=pl.BlockSpec((seq, HEAD_DIM), lambda b, h: (b, h)),
        out_shape=jax.ShapeDtypeStruct((t, d), BF16),
        scratch_shapes=[pltpu.VMEM((LANES, HEAD_DIM), F32)],
        compiler_params=_params(8 * seq * HEAD_DIM * 2 + 16 * blk * blk * 4, 2),
        name="moba_attention",
    )(qkv, qkv, qkv)


def _proj_res_kernel(*refs, has_bias):
    if has_bias:
        a_ref, w_ref, r_ref, g_ref, b_ref, o_ref, wb_ref = refs
    else:
        a_ref, w_ref, r_ref, g_ref, o_ref, wb_ref = refs

    @pl.when(pl.program_id(1) == 0)
    def _():
        wb_ref[...] = w_ref[...].astype(BF16)

    y = _dot(a_ref[...], wb_ref[...])
    if has_bias:
        y = y + b_ref[...]
    o_ref[...] = r_ref[...] + g_ref[...] * y


def _proj_residual(a, w3, widx, resid, mod, gate_part, bias=None, tm=512, tn=512):
    t, k = a.shape
    n = w3.shape[2]
    in_specs = [
        pl.BlockSpec((tm, k), lambda j, i: (i, 0)),
        pl.BlockSpec((None, k, tn), lambda j, i: (widx, 0, j)),
        pl.BlockSpec((tm, tn), lambda j, i: (i, j)),
        pl.BlockSpec((None, 1, tn), lambda j, i: (mod.row(gate_part, i * tm), 0, j)),
    ]
    args = [a, w3, resid, mod.table]
    if bias is not None:
        in_specs.append(pl.BlockSpec((1, tn), lambda j, i: (0, j)))
        args.append(bias.reshape(1, n))
    return pl.pallas_call(
        functools.partial(_proj_res_kernel, has_bias=bias is not None),
        grid=(n // tn, t // tm),
        in_specs=in_specs,
        out_specs=pl.BlockSpec((tm, tn), lambda j, i: (i, j)),
        out_shape=jax.ShapeDtypeStruct((t, n), F32),
        scratch_shapes=[pltpu.VMEM((k, tn), BF16)],
        compiler_params=_params(
            2 * tm * k * 2 + 2 * k * tn * 4 + k * tn * 2 + 5 * tm * tn * 4, 2),
        name="proj_residual",
    )(*args)


def _shift_rows(p, halo, k):
    hr = halo.shape[0]
    body = pltpu.roll(p, k, 0)
    head = jnp.where(lax.broadcasted_iota(I32, (hr, p.shape[1]), 0) < k,
                     pltpu.roll(halo, k, 0), body[:hr])
    return head, body


def _shortconv_kernel(x_ref, wb_g, wc_g, wx_g, cw_ref, o_ref, wb_ref, halo_ref, *, width, tiles_per_seq):
    i = pl.program_id(1)

    @pl.when(i == 0)
    def _():
        wb_ref[0] = wb_g[...].astype(BF16)
        wb_ref[1] = wc_g[...].astype(BF16)
        wb_ref[2] = wx_g[...].astype(BF16)

    @pl.when(i % tiles_per_seq == 0)
    def _():
        halo_ref[...] = jnp.zeros(halo_ref.shape, F32)

    x = x_ref[...]
    gb = _dot(x, wb_ref[0])
    p = _dot(x, wb_ref[1]) * _dot(x, wb_ref[2])
    hr = halo_ref.shape[0]
    halo = halo_ref[...]
    cw = cw_ref[...]
    body = cw[width - 1:width, :] * p
    head = body[:hr]
    for s in range(1, width):
        hs, bs = _shift_rows(p, halo, s)
        w = cw[width - 1 - s:width - s, :]
        body = body + w * bs
        head = head + w * hs
    o_ref[...] = (gb * body).astype(o_ref.dtype)
    o_ref[0:hr, :] = (gb[:hr] * head).astype(o_ref.dtype)
    halo_ref[...] = p[p.shape[0] - hr:, :]


def _short_conv_in(h, w_in, conv_w, seq, tm=512, tn=256):
    t, d = h.shape
    width = conv_w.shape[0]
    nd = d // tn
    hr = 16
    assert width - 1 <= hr
    kern = functools.partial(_shortconv_kernel, width=width, tiles_per_seq=seq // tm)
    wspec = lambda off: pl.BlockSpec((d, tn), lambda j, i: (0, off * nd + j))
    return pl.pallas_call(
        kern,
        grid=(nd, t // tm),
        in_specs=[
            pl.BlockSpec((tm, d), lambda j, i: (i, 0)),
            wspec(0), wspec(1), wspec(2),
            pl.BlockSpec((width, tn), lambda j, i: (0, j)),
        ],
        out_specs=pl.BlockSpec((tm, tn), lambda j, i: (i, j)),
        out_shape=jax.ShapeDtypeStruct((t, d), BF16),
        scratch_shapes=[pltpu.VMEM((3, d, tn), BF16), pltpu.VMEM((hr, tn), F32)],
        compiler_params=_params(
            2 * tm * d * 2 + 6 * d * tn * 4 + 3 * d * tn * 2 + 10 * tm * tn * 4, 2),
        name="short_conv_in",
    )(h, w_in, w_in, w_in, conv_w)


def _pool_kernel(x_ref, xh_ref, gn_ref, sc_ref, sh_ref, w_ref, ps_ref, ga_ref, o_ref,
                 wb_ref, ext_ref, *, windows, tiles_per_seq):
    i = pl.program_id(0)
    tm, d = x_ref.shape
    hr = xh_ref.shape[0]
    cg = d // len(windows)

    @pl.when(i == 0)
    def _():
        wb_ref[...] = w_ref[...].astype(BF16)

    x = x_ref[...]
    gn, sc, sh = gn_ref[...], sc_ref[...], sh_ref[...]
    h = _adaln(x, gn, sc, sh)
    hh = _adaln(xh_ref[...], gn, sc, sh)
    first = i % tiles_per_seq == 0
    ext_ref[0:hr, :] = jnp.where(first, 0.0, hh)
    ext_ref[hr:, :] = h
    pos = (i % tiles_per_seq) * tm + lax.broadcasted_iota(I32, (tm, 1), 0)
    for g, w in enumerate(windows):
        cols = slice(g * cg, (g + 1) * cg)
        s = ext_ref[:, cols]
        step = 1
        while step < w:
            s = s + pltpu.roll(s, step, 0)
            step *= 2
        count = jnp.minimum(pos + 1, w).astype(F32)
        pooled = s[hr:] / count - h[:, cols]
        mixed = _dot(pooled.astype(BF16), wb_ref[g])
        o_ref[:, cols] = x[:, cols] + ga_ref[:, cols] * (mixed * ps_ref[:, cols])


def _pool_mixer(x, gain, mod, w_group, scale, seq, tm=256):
    t, d = x.shape
    g, cg, _ = w_group.shape
    assert g == len(POOL_WINDOWS) and cg % LANES == 0
    hr = 16
    assert max(POOL_WINDOWS) <= hr and all(w & (w - 1) == 0 for w in POOL_WINDOWS)
    kern = functools.partial(_pool_kernel, windows=POOL_WINDOWS, tiles_per_seq=seq // tm)
    row = lambda p: pl.BlockSpec((None, 1, d), lambda i: (mod.row(p, i * tm), 0, 0))
    return pl.pallas_call(
        kern,
        grid=(t // tm,),
        in_specs=[
            pl.BlockSpec((tm, d), lambda i: (i, 0)),
            pl.BlockSpec((hr, d), lambda i: (jnp.maximum(i * (tm // hr) - 1, 0), 0)),
            pl.BlockSpec((1, d), lambda i: (0, 0)),
            row(1), row(0),
            pl.BlockSpec((g, cg, cg), lambda i: (0, 0, 0)),
            pl.BlockSpec((1, d), lambda i: (0, 0)),
            row(2),
        ],
        out_specs=pl.BlockSpec((tm, d), lambda i: (i, 0)),
        out_shape=jax.ShapeDtypeStruct((t, d), F32),
        scratch_shapes=[pltpu.VMEM((g, cg, cg), BF16), pltpu.VMEM((tm + hr, d), F32)],
        compiler_params=_params(4 * tm * d * 4 + 2 * g * cg * cg * 4 + g * cg * cg * 2
                                + 6 * tm * d * 4, 1),
        name="pool_mixer",
    )(x, x, gain.reshape(1, d), mod.table, mod.table, w_group, scale.reshape(1, d), mod.table)


def _glu_kernel(x_ref, wa_ref, wg_ref, ba_ref, bg_ref, o_ref, wb_ref):
    @pl.when(pl.program_id(1) == 0)
    def _():
        wb_ref[0] = wa_ref[...].astype(BF16)
        wb_ref[1] = wg_ref[...].astype(BF16)

    x = x_ref[...]
    a = _dot(x, wb_ref[0]) + ba_ref[...]
    g = _dot(x, wb_ref[1]) + bg_ref[...]
    o_ref[...] = a * jax.nn.sigmoid(g)


def _conformer_glu(h, w1, b1, tm=512, tn=512):
    t, d = h.shape
    n = w1.shape[1] // 2
    nd = n // tn
    b1 = b1.reshape(1, 2 * n)
    return pl.pallas_call(
        _glu_kernel,
        grid=(nd, t // tm),
        in_specs=[
            pl.BlockSpec((tm, d), lambda j, i: (i, 0)),
            pl.BlockSpec((d, tn), lambda j, i: (0, j)),
            pl.BlockSpec((d, tn), lambda j, i: (0, nd + j)),
            pl.BlockSpec((1, tn), lambda j, i: (0, j)),
            pl.BlockSpec((1, tn), lambda j, i: (0, nd + j)),
        ],
        out_specs=pl.BlockSpec((tm, tn), lambda j, i: (i, j)),
        out_shape=jax.ShapeDtypeStruct((t, n), F32),
        scratch_shapes=[pltpu.VMEM((2, d, tn), BF16)],
        compiler_params=_params(
            2 * tm * d * 2 + 4 * d * tn * 4 + 2 * d * tn * 2 + 6 * tm * tn * 4, 2),
        name="conformer_glu",
    )(h, w1, w1, b1, b1)


def _cfconv_kernel(u_ref, uh_ref, dw_ref, dwb_ref, lg_ref, lb_ref, o_ref, ext_ref, y_ref,
                   *, width, tiles_per_seq, rc, cc):
    i = pl.program_id(0)
    tm, d = u_ref.shape
    hr = uh_ref.shape[0]
    first = i % tiles_per_seq == 0
    ext_ref[0:hr, :] = jnp.where(first, 0.0, uh_ref[...])
    ext_ref[hr:, :] = u_ref[...]
    off = hr - (width - 1)
    for c0 in range(0, d, cc):
        w = dw_ref[:, c0:c0 + cc]
        for r0 in range(0, tm, rc):
            acc = jnp.zeros((rc, cc), F32) + dwb_ref[:, c0:c0 + cc]
            for k in range(width):
                acc = acc + w[k:k + 1, :] * ext_ref[r0 + off + k:r0 + off + k + rc, c0:c0 + cc]
            y_ref[r0:r0 + rc, c0:c0 + cc] = acc
    y = y_ref[...]
    mu = jnp.mean(y, axis=-1, keepdims=True)
    yc = y - mu
    var = jnp.mean(yc * yc, axis=-1, keepdims=True)
    z = yc * lax.rsqrt(var + LN_EPS) * lg_ref[...] + lb_ref[...]
    o_ref[...] = _silu(z).astype(o_ref.dtype)


def _conformer_conv(u, dw, dw_b, ln_g, ln_b, seq, tm=256):
    t, d = u.shape
    width = dw.shape[0]
    hr = 32
    assert width - 1 <= hr
    kern = functools.partial(_cfconv_kernel, width=width, tiles_per_seq=seq // tm,
                             rc=min(64, tm), cc=min(256, d))
    vec = lambda: pl.BlockSpec((1, d), lambda i: (0, 0))
    return pl.pallas_call(
        kern,
        grid=(t // tm,),
        in_specs=[
            pl.BlockSpec((tm, d), lambda i: (i, 0)),
            pl.BlockSpec((hr, d), lambda i: (jnp.maximum(i * (tm // hr) - 1, 0), 0)),
            pl.BlockSpec((width, d), lambda i: (0, 0)),
            vec(), vec(), vec(),
        ],
        out_specs=pl.BlockSpec((tm, d), lambda i: (i, 0)),
        out_shape=jax.ShapeDtypeStruct((t, d), BF16),
        scratch_shapes=[pltpu.VMEM((tm + hr, d), F32), pltpu.VMEM((tm, d), F32)],
        compiler_params=_params(8 * tm * d * 4, 1),
        name="conformer_conv",
    )(u, u, dw, dw_b.reshape(1, d), ln_g.reshape(1, d), ln_b.reshape(1, d))


def _gateup_kernel(te_ref, nt_ref, x_ref, wg_ref, wu_ref, o_ref, wb_ref):
    i = pl.program_id(1)
    new_w = (i == 0) | (te_ref[i] != te_ref[jnp.maximum(i - 1, 0)])

    @pl.when(new_w)
    def _():
        wb_ref[0] = wg_ref[...].astype(BF16)
        wb_ref[1] = wu_ref[...].astype(BF16)

    @pl.when(i < nt_ref[0])
    def _():
        x = x_ref[...]
        g = _dot(x, wb_ref[0])
        u = _dot(x, wb_ref[1])
        o_ref[...] = (_silu(g) * u).astype(o_ref.dtype)

    @pl.when(i >= nt_ref[0])
    def _():
        o_ref[...] = jnp.zeros(o_ref.shape, o_ref.dtype)


def _grouped_gateup(xs, wg, wu, tile_expert, n_tiles, tm, tf=512):
    p, d = xs.shape
    f = wg.shape[2]
    xmap = lambda j, i, te, nt: (jnp.minimum(i, nt[0] - 1), 0)
    wmap = lambda j, i, te, nt: (te[i], 0, j)
    return pl.pallas_call(
        _gateup_kernel,
        grid_spec=pltpu.PrefetchScalarGridSpec(
            num_scalar_prefetch=2,
            grid=(f // tf, p // tm),
            in_specs=[
                pl.BlockSpec((tm, d), xmap),
                pl.BlockSpec((None, d, tf), wmap),
                pl.BlockSpec((None, d, tf), wmap),
            ],
            out_specs=pl.BlockSpec((tm, tf), lambda j, i, te, nt: (i, j)),
            scratch_shapes=[pltpu.VMEM((2, d, tf), BF16)],
        ),
        out_shape=jax.ShapeDtypeStruct((p, f), BF16),
        compiler_params=_params(
            2 * tm * d * 2 + 4 * d * tf * 4 + 2 * d * tf * 2 + 2 * tm * tf * 2 + 4 * tm * tf * 4, 2),
        name="grouped_gateup",
    )(tile_expert, n_tiles, xs, wg, wu)


def _down_kernel(te_ref, nt_ref, a_ref, w_ref, o_ref, wb_ref):
    i = pl.program_id(1)
    new_w = (i == 0) | (te_ref[i] != te_ref[jnp.maximum(i - 1, 0)])

    @pl.when(new_w)
    def _():
        wb_ref[...] = w_ref[...].astype(BF16)

    @pl.when(i < nt_ref[0])
    def _():
        o_ref[...] = _dot(a_ref[...], wb_ref[...])

    @pl.when(i >= nt_ref[0])
    def _():
        o_ref[...] = jnp.zeros(o_ref.shape, o_ref.dtype)


def _grouped_down(act, wd, tile_expert, n_tiles, tm, tn=512):
    p, f = act.shape
    n = wd.shape[2]
    return pl.pallas_call(
        _down_kernel,
        grid_spec=pltpu.PrefetchScalarGridSpec(
            num_scalar_prefetch=2,
            grid=(n // tn, p // tm),
            in_specs=[
                pl.BlockSpec((tm, f), lambda j, i, te, nt: (jnp.minimum(i, nt[0] - 1), 0)),
                pl.BlockSpec((None, f, tn), lambda j, i, te, nt: (te[i], 0, j)),
            ],
            out_specs=pl.BlockSpec((tm, tn), lambda j, i, te, nt: (i, j)),
            scratch_shapes=[pltpu.VMEM((f, tn), BF16)],
        ),
        out_shape=jax.ShapeDtypeStruct((p, n), F32),
        compiler_params=_params(
            2 * tm * f * 2 + 2 * f * tn * 4 + f * tn * 2 + 3 * tm * tn * 4, 2),
        name="grouped_down",
    )(tile_expert, n_tiles, act, wd)


def _dense_ffn(x, h, wg, wu, wd, widx, mod, tm=512):
    t = h.shape[0]
    nt = t // tm
    te = jnp.full((nt,), widx, I32)
    act = _grouped_gateup(h, wg, wu, te, jnp.full((1,), nt, I32), tm)
    return _proj_residual(act, wd, widx, x, mod, 5, tm=tm, tn=512)


def _router_kernel(x_ref, gn_ref, sc_ref, sh_ref, r_ref, h_ref, info_ref, cnt_ref, run_ref, *, n_exp):
    i = pl.program_id(0)
    tm = x_ref.shape[0]

    @pl.when(i == 0)
    def _():
        run_ref[...] = jnp.zeros(run_ref.shape, F32)

    h = _adaln(x_ref[...], gn_ref[...], sc_ref[...], sh_ref[...])
    h_ref[...] = h
    h_hi = h.astype(BF16)
    h_lo = (h - h_hi.astype(F32)).astype(BF16)
    r = r_ref[...]
    r_hi = r.astype(BF16)
    r_lo = (r - r_hi.astype(F32)).astype(BF16)
    logits = _dot(h_hi, r_hi) + (_dot(h_lo, r_hi) + _dot(h_hi, r_lo))

    lane = lax.broadcasted_iota(I32, (tm, LANES), 1).astype(F32)
    lg = jnp.where(lane < n_exp, logits, -jnp.inf)
    v1 = jnp.max(lg, axis=1, keepdims=True)
    i1 = jnp.min(jnp.where(lg == v1, lane, float(LANES)), axis=1, keepdims=True)
    lg2 = jnp.where(lane == i1, -jnp.inf, lg)
    v2 = jnp.max(lg2, axis=1, keepdims=True)
    i2 = jnp.min(jnp.where(lg2 == v2, lane, float(LANES)), axis=1, keepdims=True)
    e = jnp.exp(v2 - v1)
    g1 = 1.0 / (1.0 + e)
    g2 = e / (1.0 + e)

    oh = jnp.where((lane == i1) | (lane == i2), 1.0, 0.0)
    tri = jnp.where(lax.broadcasted_iota(I32, (tm, tm), 0) > lax.broadcasted_iota(I32, (tm, tm), 1),
                    1.0, 0.0).astype(BF16)
    cum = _dot(tri, oh.astype(BF16)) + run_ref[...]
    rank1 = jnp.sum(jnp.where(lane == i1, cum, 0.0), axis=1, keepdims=True)
    rank2 = jnp.sum(jnp.where(lane == i2, cum, 0.0), axis=1, keepdims=True)
    run = run_ref[...] + jnp.sum(oh, axis=0, keepdims=True)
    run_ref[...] = run
    cnt_ref[...] = jnp.broadcast_to(run, cnt_ref.shape)

    info = jnp.zeros((tm, LANES), F32)
    for k, val in enumerate((i1, i2, g1, g2, rank1, rank2)):
        info = jnp.where(lane == float(k), val, info)
    info_ref[...] = info


def _route(x, gain, mod, router, tm=256):
    t, d = x.shape
    n_exp = router.shape[1]
    assert n_exp <= LANES
    rpad = jnp.pad(router, ((0, 0), (0, LANES - n_exp)))
    row = lambda p: pl.BlockSpec((None, 1, d), lambda i: (mod.row(p, i * tm), 0, 0))
    return pl.pallas_call(
        functools.partial(_router_kernel, n_exp=n_exp),
        grid=(t // tm,),
        in_specs=[
            pl.BlockSpec((tm, d), lambda i: (i, 0)),
            pl.BlockSpec((1, d), lambda i: (0, 0)),
            row(4), row(3),
            pl.BlockSpec((d, LANES), lambda i: (0, 0)),
        ],
        out_specs=[
            pl.BlockSpec((tm, d), lambda i: (i, 0)),
            pl.BlockSpec((tm, LANES), lambda i: (i, 0)),
            pl.BlockSpec((8, LANES), lambda i: (0, 0)),
        ],
        out_shape=[
            jax.ShapeDtypeStruct((t, d), F32),
            jax.ShapeDtypeStruct((t, LANES), F32),
            jax.ShapeDtypeStruct((8, LANES), F32),
        ],
        scratch_shapes=[pltpu.VMEM((1, LANES), F32)],
        compiler_params=_params(4 * tm * d * 4 + 4 * tm * d * 4 + 2 * d * LANES * 4, 1),
        name="moe_route",
    )(x, gain.reshape(1, d), mod.table, mod.table, rpad)


def _gather_kernel(tok_ref, tot_ref, h_hbm, o_ref, buf_ref, sem):
    i = pl.program_id(0)
    tg = buf_ref.shape[0]
    base = i * tg

    def row_copy(r, src_row):
        return pltpu.make_async_copy(h_hbm.at[pl.ds(src_row, 1)], buf_ref.at[pl.ds(r, 1)], sem)

    @pl.when(base < tot_ref[0])
    def _():
        def start(r, c):
            row_copy(r, tok_ref[base + r]).start()
            return c

        def wait(r, c):
            row_copy(r, 0).wait()
            return c

        lax.fori_loop(0, tg, start, 0)
        lax.fori_loop(0, tg, wait, 0)
        o_ref[...] = buf_ref[...].astype(o_ref.dtype)

    @pl.when(base >= tot_ref[0])
    def _():
        o_ref[...] = jnp.zeros(o_ref.shape, o_ref.dtype)


def _gather_rows(hf, sorted_tok, total_rows, tg=256):
    d = hf.shape[1]
    p = sorted_tok.shape[0]
    return pl.pallas_call(
        _gather_kernel,
        grid_spec=pltpu.PrefetchScalarGridSpec(
            num_scalar_prefetch=2,
            grid=(p // tg,),
            in_specs=[pl.BlockSpec(memory_space=pl.ANY)],
            out_specs=pl.BlockSpec((tg, d), lambda i, tok, tot: (i, 0)),
            scratch_shapes=[pltpu.VMEM((tg, d), F32), pltpu.SemaphoreType.DMA(())],
        ),
        out_shape=jax.ShapeDtypeStruct((p, d), BF16),
        compiler_params=_params(tg * d * 4 + 2 * tg * d * 2 + tg * d * 4, 1),
        name="moe_gather",
    )(sorted_tok, total_rows, hf)


def _combine_kernel(*refs, final):
    if final:
        pos_ref, y_hbm, x_ref, info_ref, gf_ref, nf_ref, o_ref, b0_ref, b1_ref, sem = refs
    else:
        pos_ref, y_hbm, x_ref, info_ref, gf_ref, o_ref, b0_ref, b1_ref, sem = refs
    i = pl.program_id(0)
    tc = x_ref.shape[0]
    base = i * tc

    def row_copy(buf, r, src_row):
        return pltpu.make_async_copy(y_hbm.at[pl.ds(src_row, 1)], buf.at[pl.ds(r, 1)], sem)

    def start(r, c):
        row_copy(b0_ref, r, pos_ref[2 * (base + r)]).start()
        row_copy(b1_ref, r, pos_ref[2 * (base + r) + 1]).start()
        return c

    def wait(r, c):
        row_copy(b0_ref, r, 0).wait()
        row_copy(b1_ref, r, 0).wait()
        return c

    lax.fori_loop(0, tc, start, 0)
    lax.fori_loop(0, tc, wait, 0)
    info = info_ref[...]
    y = info[:, 2:3] * b0_ref[...] + info[:, 3:4] * b1_ref[...]
    xn = x_ref[...] + gf_ref[...] * y
    if final:
        ms = jnp.mean(xn * xn, axis=-1, keepdims=True)
        xn = xn * lax.rsqrt(ms + NORM_EPS) * nf_ref[...]
    o_ref[...] = xn


def _combine(y, pos_flat, x, info, mod, final_gain=None, tc=256):
    t, d = x.shape
    final = final_gain is not None
    in_specs = [
        pl.BlockSpec(memory_space=pl.ANY),
        pl.BlockSpec((tc, d), lambda i, pos: (i, 0)),
        pl.BlockSpec((tc, LANES), lambda i, pos: (i, 0)),
        pl.BlockSpec((None, 1, d), lambda i, pos: (mod.row(5, i * tc), 0, 0)),
    ]
    args = [pos_flat, y, x, info, mod.table]
    if final:
        in_specs.append(pl.BlockSpec((1, d), lambda i, pos: (0, 0)))
        args.append(final_gain.reshape(1, d))
    return pl.pallas_call(
        functools.partial(_combine_kernel, final=final),
        grid_spec=pltpu.PrefetchScalarGridSpec(
            num_scalar_prefetch=1,
            grid=(t // tc,),
            in_specs=in_specs,
            out_specs=pl.BlockSpec((tc, d), lambda i, pos: (i, 0)),
            scratch_shapes=[pltpu.VMEM((tc, d), F32), pltpu.VMEM((tc, d), F32),
                            pltpu.SemaphoreType.DMA(())],
        ),
        out_shape=jax.ShapeDtypeStruct((t, d), F32),
        compiler_params=_params(2 * tc * d * 4 + 4 * tc * d * 4 + 3 * tc * d * 4, 1),
        name="moe_combine",
    )(*args)


def _moe_ffn(x, gain, mod, router, wg, wu, wd, widx, final_gain=None, tm=512):
    t, d = x.shape
    n_exp = router.shape[1]
    wg, wu, wd = (w.reshape((-1,) + w.shape[2:]) for w in (wg, wu, wd))
    hf, info, cnt = _route(x, gain, mod, router)

    idx = info[:, 0:2].astype(I32)
    rank = info[:, 4:6].astype(I32)
    counts = cnt[0, :n_exp].astype(I32)
    padded = ((counts + tm - 1) // tm) * tm
    ends = jnp.cumsum(padded)
    starts = ends - padded
    pos = starts[idx] + rank
    p_max = MOE_TOPK * t + n_exp * tm
    tok = jnp.repeat(jnp.arange(t, dtype=I32), MOE_TOPK)
    sorted_tok = jnp.zeros((p_max,), I32).at[pos.reshape(-1)].set(tok)
    total = ends[-1:].astype(I32)
    n_tiles = total // tm
    tile_row = jnp.minimum(jnp.arange(p_max // tm, dtype=I32), n_tiles[0] - 1) * tm
    tile_expert = widx * n_exp + jnp.minimum(
        jnp.searchsorted(ends, tile_row, side="right"), n_exp - 1).astype(I32)

    xs = _gather_rows(hf, sorted_tok, total)
    act = _grouped_gateup(xs, wg, wu, tile_expert, n_tiles, tm)
    y = _grouped_down(act, wd, tile_expert, n_tiles, tm)
    return _combine(y, pos.reshape(-1), x, info, mod, final_gain)


def kernel(x, c, mod_w, mod_b, norm_mix, norm_ffn, norm_final, moba_wqkv, moba_wo, sc_w_in, sc_conv,
           sc_w_out, pool_w, pool_scale, cf_w1, cf_b1, cf_dw, cf_dw_b, cf_ln_g, cf_ln_b, cf_w2, cf_b2,
           ffn_wg, ffn_wu, ffn_wd, moe_router, moe_wg, moe_wu, moe_wd):
    batch, seq, d = x.shape
    depth = mod_w.shape[0]
    table = _modulation(c, mod_w, mod_b)
    xt = x.reshape(batch * seq, d)
    for layer in range(depth):
        mod = _Mod(table, layer, batch, seq)
        m, j = layer % 4, layer // 4
        if m == 0:
            h = _prenorm(xt, norm_mix[layer], mod, 0)
            qkv = _qkv_rope(h, moba_wqkv[j], seq)
            o = _moba_attention(qkv, batch, seq, d)
            xt = _proj_residual(o, moba_wo, j, xt, mod, 2)
        elif m == 1:
            h = _prenorm(xt, norm_mix[layer], mod, 0)
            v = _short_conv_in(h, sc_w_in[j], sc_conv[j], seq)
            xt = _proj_residual(v, sc_w_out, j, xt, mod, 2)
        elif m == 2:
            xt = _pool_mixer(xt, norm_mix[layer], mod, pool_w[j], pool_scale[j], seq)
        else:
            h = _prenorm(xt, norm_mix[layer], mod, 0)
            u = _conformer_glu(h, cf_w1[j], cf_b1[j])
            z = _conformer_conv(u, cf_dw[j], cf_dw_b[j], cf_ln_g[j], cf_ln_b[j], seq)
            xt = _proj_residual(z, cf_w2, j, xt, mod, 2, bias=cf_b2[j])

        i = layer // 2
        if layer % 2 == 0:
            h = _prenorm(xt, norm_ffn[layer], mod, 3)
            xt = _dense_ffn(xt, h, ffn_wg, ffn_wu, ffn_wd, i, mod)
        else:
            last = layer == depth - 1
            xt = _moe_ffn(xt, norm_ffn[layer], mod, moe_router[i], moe_wg, moe_wu, moe_wd, i,
                          final_gain=norm_final if last else None)
    if depth % 2 == 1:
        xt = _final_norm(xt, norm_final)
    return xt.reshape(batch, seq, d)
```

```python
import functools

import jax
import jax.numpy as jnp
from jax import lax
from jax.experimental import pallas as pl
from jax.experimental.pallas import tpu as pltpu

F32 = jnp.float32
BF16 = jnp.bfloat16
I32 = jnp.int32
U32 = jnp.uint32

HEAD_DIM = 128
MOBA_BLOCK = 256
MOBA_TOPK = 3
ROPE_THETA = 10000.0
POOL_WINDOWS = (2, 4, 8, 16)
MOE_TOPK = 2
NORM_EPS = 1e-6
LN_EPS = 1e-5
NEG_INF = -1e30

LANES = 128
SUBLANES = 8
VMEM_LIMIT_CAP = 60000 * 1024
MOD_PARTS = 6


def _params(vmem_bytes, n_axes):
    limit = min(int(vmem_bytes) + (6 << 20), VMEM_LIMIT_CAP)
    return pltpu.CompilerParams(
        dimension_semantics=("arbitrary",) * n_axes, vmem_limit_bytes=limit)


def _dot(a, b):
    return jnp.dot(a, b, preferred_element_type=F32)


def _dot_nt(a, b):
    return lax.dot_general(a, b, (((1,), (1,)), ((), ())), preferred_element_type=F32)


def _silu(x):
    return x * jax.nn.sigmoid(x)


def _adaln(x, g, sc, sh):
    ms = jnp.mean(x * x, axis=-1, keepdims=True)
    return (x * lax.rsqrt(ms + NORM_EPS)) * g * (1.0 + sc) + sh


def _mod_kernel(c_ref, w_ref, b_ref, o_ref):
    c = c_ref[...]
    ca = _silu(c).astype(BF16)
    o_ref[...] = _dot(ca, w_ref[...].astype(BF16)) + b_ref[...]


def _modulation(c, mod_w, mod_b):
    depth, d, n = mod_w.shape
    b = c.shape[0]
    tn = 1024
    out = pl.pallas_call(
        _mod_kernel,
        grid=(depth, n // tn),
        in_specs=[
            pl.BlockSpec((b, d), lambda l, j: (0, 0)),
            pl.BlockSpec((None, d, tn), lambda l, j: (l, 0, j)),
            pl.BlockSpec((None, 1, tn), lambda l, j: (l, 0, j)),
        ],
        out_specs=pl.BlockSpec((None, b, tn), lambda l, j: (l, 0, j)),
        out_shape=jax.ShapeDtypeStruct((depth, b, n), F32),
        compiler_params=_params(2 * d * tn * 4 + d * tn * 2, 2),
        name="modulation",
    )(c, mod_w, mod_b.reshape(depth, 1, n))
    return out.reshape(depth * b * MOD_PARTS, 1, d)


class _Mod:
    def __init__(self, table, layer, batch, seq):
        self.table = table
        self.base = layer * batch * MOD_PARTS
        self.seq = seq

    def row(self, part, tok0):
        return self.base + (tok0 // self.seq) * MOD_PARTS + part


def _prenorm_kernel(x_ref, g_ref, sc_ref, sh_ref, o_ref):
    o_ref[...] = _adaln(x_ref[...], g_ref[...], sc_ref[...], sh_ref[...]).astype(o_ref.dtype)


def _prenorm(x, gain, mod, part0, tm=512):
    t, d = x.shape
    row = lambda p: pl.BlockSpec((None, 1, d), lambda i: (mod.row(p, i * tm), 0, 0))
    return pl.pallas_call(
        _prenorm_kernel,
        grid=(t // tm,),
        in_specs=[
            pl.BlockSpec((tm, d), lambda i: (i, 0)),
            pl.BlockSpec((1, d), lambda i: (0, 0)),
            row(part0 + 1), row(part0),
        ],
        out_specs=pl.BlockSpec((tm, d), lambda i: (i, 0)),
        out_shape=jax.ShapeDtypeStruct((t, d), BF16),
        compiler_params=_params(2 * tm * d * 6 + 3 * tm * d * 4, 1),
        name="prenorm",
    )(x, gain.reshape(1, d), mod.table, mod.table)


def _final_norm_kernel(x_ref, g_ref, o_ref):
    x = x_ref[...]
    ms = jnp.mean(x * x, axis=-1, keepdims=True)
    o_ref[...] = x * lax.rsqrt(ms + NORM_EPS) * g_ref[...]


def _final_norm(x, gain, tm=512):
    t, d = x.shape
    return pl.pallas_call(
        _final_norm_kernel,
        grid=(t // tm,),
        in_specs=[pl.BlockSpec((tm, d), lambda i: (i, 0)), pl.BlockSpec((1, d), lambda i: (0, 0))],
        out_specs=pl.BlockSpec((tm, d), lambda i: (i, 0)),
        out_shape=jax.ShapeDtypeStruct((t, d), F32),
        compiler_params=_params(6 * tm * d * 4, 1),
        name="final_norm",
    )(x, gain.reshape(1, d))


def _qkv_kernel(x_ref, w_ref, cos_ref, sin_ref, o_ref, wb_ref, *, nq, scale):
    j = pl.program_id(0)

    @pl.when(pl.program_id(1) == 0)
    def _():
        wb_ref[...] = w_ref[...].astype(BF16)

    acc = _dot(x_ref[...], wb_ref[...])
    tn = acc.shape[1]

    def rope(mult):
        cos = cos_ref[...]
        sin = sin_ref[...]
        for c in range(tn // HEAD_DIM):
            a = acc[:, c * HEAD_DIM:(c + 1) * HEAD_DIM]
            r = a * cos + pltpu.roll(a, HEAD_DIM // 2, 1) * sin
            if mult != 1.0:
                r = r * mult
            o_ref[:, c * HEAD_DIM:(c + 1) * HEAD_DIM] = r.astype(o_ref.dtype)

    @pl.when(j < nq)
    def _():
        rope(scale)

    @pl.when((j >= nq) & (j < 2 * nq))
    def _():
        rope(1.0)

    @pl.when(j >= 2 * nq)
    def _():
        o_ref[...] = acc.astype(o_ref.dtype)


def _qkv_rope(h, w, seq, tm=1024, tn=512):
    t, d = h.shape
    n = w.shape[1]
    half = HEAD_DIM // 2
    inv = ROPE_THETA ** (-jnp.arange(half, dtype=F32) * (2.0 / HEAD_DIM))
    ang = jnp.arange(seq, dtype=F32)[:, None] * inv[None, :]
    cos = jnp.concatenate([jnp.cos(ang), jnp.cos(ang)], axis=-1)
    sin = jnp.concatenate([-jnp.sin(ang), jnp.sin(ang)], axis=-1)
    tm = min(tm, seq)
    sblk = seq // tm
    kern = functools.partial(_qkv_kernel, nq=d // tn, scale=HEAD_DIM ** -0.5)
    return pl.pallas_call(
        kern,
        grid=(n // tn, t // tm),
        in_specs=[
            pl.BlockSpec((tm, d), lambda j, i: (i, 0)),
            pl.BlockSpec((d, tn), lambda j, i: (0, j)),
            pl.BlockSpec((tm, HEAD_DIM), lambda j, i: (i % sblk, 0)),
            pl.BlockSpec((tm, HEAD_DIM), lambda j, i: (i % sblk, 0)),
        ],
        out_specs=pl.BlockSpec((tm, tn), lambda j, i: (i, j)),
        out_shape=jax.ShapeDtypeStruct((t, n), BF16),
        scratch_shapes=[pltpu.VMEM((d, tn), BF16)],
        compiler_params=_params(
            2 * tm * d * 2 + 2 * d * tn * 4 + d * tn * 2 + 2 * tm * tn * 2 + 3 * tm * tn * 4, 2),
        name="qkv_rope",
    )(h, w, cos, sin)


def _moba_kernel(q_ref, k_ref, v_ref, o_ref, kbar_ref, vt_ref, s_ref, p_ref, *, nb, blk, topk):
    kr = kbar_ref.shape[0]

    kbar_ref[...] = jnp.zeros(kbar_ref.shape, F32)
    for m in range(nb):
        rows = slice(m * blk, (m + 1) * blk)
        kbar_ref[m:m + 1, :] = jnp.mean(k_ref[rows, :].astype(F32), axis=0, keepdims=True)
        vt_ref[:, rows] = v_ref[rows, :].astype(F32).T.astype(BF16)
    kbar = kbar_ref[...]
    kb_hi = kbar.astype(BF16)
    kb_lo = (kbar - kb_hi.astype(F32)).astype(BF16)

    blk_id = lax.broadcasted_iota(I32, (kr, blk), 0)
    causal = lax.broadcasted_iota(I32, (blk, blk), 0) <= lax.broadcasted_iota(I32, (blk, blk), 1)

    for n in range(nb):
        q = q_ref[n * blk:(n + 1) * blk, :]
        if n > 0:
            gate = _dot_nt(kb_hi, q) + _dot_nt(kb_lo, q)
            valid = blk_id < n
            g = jnp.where(valid, gate, NEG_INF)
            cnt = jnp.zeros((kr, blk), F32)
            for mp in range(n):
                other = g[mp:mp + 1, :]
                cnt = cnt + jnp.where(other > g, 1.0, jnp.where((other == g) & (blk_id > mp), 1.0, 0.0))
            sel = jnp.where(valid & (cnt < topk), 1.0, 0.0)

        mx = None
        for m in range(n + 1):
            rows = slice(m * blk, (m + 1) * blk)
            s = _dot_nt(k_ref[rows, :], q)
            s = jnp.where(causal if m == n else sel[m:m + 1, :] > 0.0, s, NEG_INF)
            s_ref[rows, :] = s
            bm = jnp.max(s, axis=0, keepdims=True)
            mx = bm if mx is None else jnp.maximum(mx, bm)
        denom = jnp.zeros((1, blk), F32)
        for m in range(n + 1):
            rows = slice(m * blk, (m + 1) * blk)
            p = jnp.exp(s_ref[rows, :] - mx)
            denom = denom + jnp.sum(p, axis=0, keepdims=True)
            p_ref[rows, :] = p.astype(BF16)
        keys = (n + 1) * blk
        acc = _dot(vt_ref[:, 0:keys], p_ref[0:keys, :])
        o_ref[n * blk:(n + 1) * blk, :] = (acc / denom).T.astype(o_ref.dtype)


def _moba_attention(qkv, batch, seq, d):
    t = qkv.shape[0]
    heads = d // HEAD_DIM
    blk = MOBA_BLOCK
    assert seq % blk == 0
    nb = seq // blk
    kr = 16
    assert nb <= kr
    kern = functools.partial(_moba_kernel, nb=nb, blk=blk, topk=min(MOBA_TOPK, nb - 1))
    return pl.pallas_call(
        kern,
        grid=(batch, heads),
        in_specs=[
            pl.BlockSpec((seq, HEAD_DIM), lambda b, h: (b, h)),
            pl.BlockSpec((seq, HEAD_DIM), lambda b, h: (b, heads + h)),
            pl.BlockSpec((seq, HEAD_DIM), lambda b, h: (b, 2 * heads + h)),
        ],
        out_specs=pl.BlockSpec((seq, HEAD_DIM), lambda b, h: (b, h)),
        out_shape=jax.ShapeDtypeStruct((t, d), BF16),
        scratch_shapes=[pltpu.VMEM((kr, HEAD_DIM), F32), pltpu.VMEM((HEAD_DIM, seq), BF16),
                        pltpu.VMEM((seq, blk), F32), pltpu.VMEM((seq, blk), BF16)],
        compiler_params=_params(10 * seq * HEAD_DIM * 2 + seq * blk * 6 + 16 * blk * blk * 4, 2),
        name="moba_attention",
    )(qkv, qkv, qkv)


def _proj_res_kernel(*refs, has_bias):
    if has_bias:
        a_ref, w_ref, r_ref, g_ref, b_ref, o_ref, wb_ref = refs
    else:
        a_ref, w_ref, r_ref, g_ref, o_ref, wb_ref = refs

    @pl.when(pl.program_id(1) == 0)
    def _():
        wb_ref[...] = w_ref[...].astype(BF16)

    y = _dot(a_ref[...], wb_ref[...])
    if has_bias:
        y = y + b_ref[...]
    o_ref[...] = r_ref[...] + g_ref[...] * y


def _proj_residual(a, w3, widx, resid, mod, gate_part, bias=None, tm=1024, tn=512):
    t, k = a.shape
    n = w3.shape[2]
    in_specs = [
        pl.BlockSpec((tm, k), lambda j, i: (i, 0)),
        pl.BlockSpec((None, k, tn), lambda j, i: (widx, 0, j)),
        pl.BlockSpec((tm, tn), lambda j, i: (i, j)),
        pl.BlockSpec((None, 1, tn), lambda j, i: (mod.row(gate_part, i * tm), 0, j)),
    ]
    args = [a, w3, resid, mod.table]
    if bias is not None:
        in_specs.append(pl.BlockSpec((1, tn), lambda j, i: (0, j)))
        args.append(bias.reshape(1, n))
    return pl.pallas_call(
        functools.partial(_proj_res_kernel, has_bias=bias is not None),
        grid=(n // tn, t // tm),
        in_specs=in_specs,
        out_specs=pl.BlockSpec((tm, tn), lambda j, i: (i, j)),
        out_shape=jax.ShapeDtypeStruct((t, n), F32),
        scratch_shapes=[pltpu.VMEM((k, tn), BF16)],
        compiler_params=_params(
            2 * tm * k * 2 + 2 * k * tn * 4 + k * tn * 2 + 5 * tm * tn * 4, 2),
        name="proj_residual",
    )(*args)


def _shift_rows(p, halo, k):
    hr = halo.shape[0]
    body = pltpu.roll(p, k, 0)
    head = jnp.where(lax.broadcasted_iota(I32, (hr, p.shape[1]), 0) < k,
                     pltpu.roll(halo, k, 0), body[:hr])
    return head, body


def _shortconv_kernel(x_ref, wb_g, wc_g, wx_g, cw_ref, o_ref, wb_ref, halo_ref, *, width, tiles_per_seq):
    i = pl.program_id(1)

    @pl.when(i == 0)
    def _():
        wb_ref[0] = wb_g[...].astype(BF16)
        wb_ref[1] = wc_g[...].astype(BF16)
        wb_ref[2] = wx_g[...].astype(BF16)

    @pl.when(i % tiles_per_seq == 0)
    def _():
        halo_ref[...] = jnp.zeros(halo_ref.shape, F32)

    x = x_ref[...]
    gb = _dot(x, wb_ref[0])
    p = _dot(x, wb_ref[1]) * _dot(x, wb_ref[2])
    hr = halo_ref.shape[0]
    halo = halo_ref[...]
    cw = cw_ref[...]
    body = cw[width - 1:width, :] * p
    head = body[:hr]
    for s in range(1, width):
        hs, bs = _shift_rows(p, halo, s)
        w = cw[width - 1 - s:width - s, :]
        body = body + w * bs
        head = head + w * hs
    o_ref[...] = (gb * body).astype(o_ref.dtype)
    o_ref[0:hr, :] = (gb[:hr] * head).astype(o_ref.dtype)
    halo_ref[...] = p[p.shape[0] - hr:, :]


def _short_conv_in(h, w_in, conv_w, seq, tm=1024, tn=256):
    t, d = h.shape
    width = conv_w.shape[0]
    nd = d // tn
    hr = 16
    assert width - 1 <= hr
    kern = functools.partial(_shortconv_kernel, width=width, tiles_per_seq=seq // tm)
    wspec = lambda off: pl.BlockSpec((d, tn), lambda j, i: (0, off * nd + j))
    return pl.pallas_call(
        kern,
        grid=(nd, t // tm),
        in_specs=[
            pl.BlockSpec((tm, d), lambda j, i: (i, 0)),
            wspec(0), wspec(1), wspec(2),
            pl.BlockSpec((width, tn), lambda j, i: (0, j)),
        ],
        out_specs=pl.BlockSpec((tm, tn), lambda j, i: (i, j)),
        out_shape=jax.ShapeDtypeStruct((t, d), BF16),
        scratch_shapes=[pltpu.VMEM((3, d, tn), BF16), pltpu.VMEM((hr, tn), F32)],
        compiler_params=_params(
            2 * tm * d * 2 + 6 * d * tn * 4 + 3 * d * tn * 2 + 10 * tm * tn * 4, 2),
        name="short_conv_in",
    )(h, w_in, w_in, w_in, conv_w)


def _pool_kernel(x_ref, xh_ref, gn_ref, sc_ref, sh_ref, w_ref, ps_ref, ga_ref, o_ref,
                 wb_ref, ext_ref, *, windows, tiles_per_seq):
    i = pl.program_id(0)
    tm, d = x_ref.shape
    hr = xh_ref.shape[0]
    cg = d // len(windows)

    @pl.when(i == 0)
    def _():
        wb_ref[...] = w_ref[...].astype(BF16)

    x = x_ref[...]
    gn, sc, sh = gn_ref[...], sc_ref[...], sh_ref[...]
    h = _adaln(x, gn, sc, sh)
    hh = _adaln(xh_ref[...], gn, sc, sh)
    first = i % tiles_per_seq == 0
    ext_ref[0:hr, :] = jnp.where(first, 0.0, hh)
    ext_ref[hr:, :] = h
    pos = (i % tiles_per_seq) * tm + lax.broadcasted_iota(I32, (tm, 1), 0)
    for g, w in enumerate(windows):
        cols = slice(g * cg, (g + 1) * cg)
        s = ext_ref[:, cols]
        step = 1
        while step < w:
            s = s + pltpu.roll(s, step, 0)
            step *= 2
        count = jnp.minimum(pos + 1, w).astype(F32)
        pooled = s[hr:] / count - h[:, cols]
        mixed = _dot(pooled.astype(BF16), wb_ref[g])
        o_ref[:, cols] = x[:, cols] + ga_ref[:, cols] * (mixed * ps_ref[:, cols])


def _pool_mixer(x, gain, mod, w_group, scale, seq, tm=256):
    t, d = x.shape
    g, cg, _ = w_group.shape
    assert g == len(POOL_WINDOWS) and cg % LANES == 0
    hr = 16
    assert max(POOL_WINDOWS) <= hr and all(w & (w - 1) == 0 for w in POOL_WINDOWS)
    kern = functools.partial(_pool_kernel, windows=POOL_WINDOWS, tiles_per_seq=seq // tm)
    row = lambda p: pl.BlockSpec((None, 1, d), lambda i: (mod.row(p, i * tm), 0, 0))
    return pl.pallas_call(
        kern,
        grid=(t // tm,),
        in_specs=[
            pl.BlockSpec((tm, d), lambda i: (i, 0)),
            pl.BlockSpec((hr, d), lambda i: (jnp.maximum(i * (tm // hr) - 1, 0), 0)),
            pl.BlockSpec((1, d), lambda i: (0, 0)),
            row(1), row(0),
            pl.BlockSpec((g, cg, cg), lambda i: (0, 0, 0)),
            pl.BlockSpec((1, d), lambda i: (0, 0)),
            row(2),
        ],
        out_specs=pl.BlockSpec((tm, d), lambda i: (i, 0)),
        out_shape=jax.ShapeDtypeStruct((t, d), F32),
        scratch_shapes=[pltpu.VMEM((g, cg, cg), BF16), pltpu.VMEM((tm + hr, d), F32)],
        compiler_params=_params(4 * tm * d * 4 + 2 * g * cg * cg * 4 + g * cg * cg * 2
                                + 6 * tm * d * 4, 1),
        name="pool_mixer",
    )(x, x, gain.reshape(1, d), mod.table, mod.table, w_group, scale.reshape(1, d), mod.table)


def _glu_kernel(x_ref, wa_ref, wg_ref, ba_ref, bg_ref, o_ref, wb_ref):
    @pl.when(pl.program_id(1) == 0)
    def _():
        wb_ref[0] = wa_ref[...].astype(BF16)
        wb_ref[1] = wg_ref[...].astype(BF16)

    x = x_ref[...]
    a = _dot(x, wb_ref[0]) + ba_ref[...]
    g = _dot(x, wb_ref[1]) + bg_ref[...]
    o_ref[...] = a * jax.nn.sigmoid(g)


def _conformer_glu(h, w1, b1, tm=1024, tn=512):
    t, d = h.shape
    n = w1.shape[1] // 2
    nd = n // tn
    b1 = b1.reshape(1, 2 * n)
    return pl.pallas_call(
        _glu_kernel,
        grid=(nd, t // tm),
        in_specs=[
            pl.BlockSpec((tm, d), lambda j, i: (i, 0)),
            pl.BlockSpec((d, tn), lambda j, i: (0, j)),
            pl.BlockSpec((d, tn), lambda j, i: (0, nd + j)),
            pl.BlockSpec((1, tn), lambda j, i: (0, j)),
            pl.BlockSpec((1, tn), lambda j, i: (0, nd + j)),
        ],
        out_specs=pl.BlockSpec((tm, tn), lambda j, i: (i, j)),
        out_shape=jax.ShapeDtypeStruct((t, n), F32),
        scratch_shapes=[pltpu.VMEM((2, d, tn), BF16)],
        compiler_params=_params(
            2 * tm * d * 2 + 4 * d * tn * 4 + 2 * d * tn * 2 + 6 * tm * tn * 4, 2),
        name="conformer_glu",
    )(h, w1, w1, b1, b1)


def _cfconv_kernel(u_ref, uh_ref, dw_ref, dwb_ref, lg_ref, lb_ref, o_ref, ph_ref, y_ref,
                   *, width, tiles_per_seq, rc, cc):
    i = pl.program_id(0)
    tm, d = u_ref.shape
    hr = uh_ref.shape[0]
    rows = tm + hr
    first = i % tiles_per_seq == 0
    off = hr - (width - 1)
    for c0 in range(0, d, cc):
        cols = slice(c0, c0 + cc)
        ph_ref[0, 0:hr, :] = jnp.where(first, 0.0, uh_ref[:, cols])
        ph_ref[0, hr:, :] = u_ref[:, cols]
        e0 = ph_ref[0]
        for s in range(1, SUBLANES):
            ph_ref[s] = pltpu.roll(e0, rows - s, 0)
        w = dw_ref[:, cols]
        for r0 in range(0, tm, rc):
            acc = jnp.zeros((rc, cc), F32) + dwb_ref[:, cols]
            for k in range(width):
                q, s = divmod(off + k, SUBLANES)
                a0 = r0 + q * SUBLANES
                acc = acc + w[k:k + 1, :] * ph_ref[s, a0:a0 + rc, :]
            y_ref[r0:r0 + rc, cols] = acc
    y = y_ref[...]
    mu = jnp.mean(y, axis=-1, keepdims=True)
    yc = y - mu
    var = jnp.mean(yc * yc, axis=-1, keepdims=True)
    z = yc * lax.rsqrt(var + LN_EPS) * lg_ref[...] + lb_ref[...]
    o_ref[...] = _silu(z).astype(o_ref.dtype)


def _conformer_conv(u, dw, dw_b, ln_g, ln_b, seq, tm=256):
    t, d = u.shape
    width = dw.shape[0]
    hr = 32
    assert width - 1 <= hr
    cc = min(256, d)
    kern = functools.partial(_cfconv_kernel, width=width, tiles_per_seq=seq // tm,
                             rc=min(64, tm), cc=cc)
    vec = lambda: pl.BlockSpec((1, d), lambda i: (0, 0))
    return pl.pallas_call(
        kern,
        grid=(t // tm,),
        in_specs=[
            pl.BlockSpec((tm, d), lambda i: (i, 0)),
            pl.BlockSpec((hr, d), lambda i: (jnp.maximum(i * (tm // hr) - 1, 0), 0)),
            pl.BlockSpec((width, d), lambda i: (0, 0)),
            vec(), vec(), vec(),
        ],
        out_specs=pl.BlockSpec((tm, d), lambda i: (i, 0)),
        out_shape=jax.ShapeDtypeStruct((t, d), BF16),
        scratch_shapes=[pltpu.VMEM((SUBLANES, tm + hr, cc), F32), pltpu.VMEM((tm, d), F32)],
        compiler_params=_params(8 * tm * d * 4 + SUBLANES * (tm + hr) * cc * 4, 1),
        name="conformer_conv",
    )(u, u, dw, dw_b.reshape(1, d), ln_g.reshape(1, d), ln_b.reshape(1, d))


def _gateup_kernel(te_ref, nt_ref, x_ref, wg_ref, wu_ref, o_ref, wb_ref):
    i = pl.program_id(1)
    new_w = (i == 0) | (te_ref[i] != te_ref[jnp.maximum(i - 1, 0)])

    @pl.when(new_w)
    def _():
        wb_ref[0] = wg_ref[...].astype(BF16)
        wb_ref[1] = wu_ref[...].astype(BF16)

    @pl.when(i < nt_ref[0])
    def _():
        x = x_ref[...]
        g = _dot(x, wb_ref[0])
        u = _dot(x, wb_ref[1])
        o_ref[...] = (_silu(g) * u).astype(o_ref.dtype)

    @pl.when(i >= nt_ref[0])
    def _():
        o_ref[...] = jnp.zeros(o_ref.shape, o_ref.dtype)


def _grouped_gateup(xs, wg, wu, tile_expert, n_tiles, tm, tf=512):
    p, d = xs.shape
    f = wg.shape[2]
    xmap = lambda j, i, te, nt: (jnp.minimum(i, nt[0] - 1), 0)
    wmap = lambda j, i, te, nt: (te[i], 0, j)
    return pl.pallas_call(
        _gateup_kernel,
        grid_spec=pltpu.PrefetchScalarGridSpec(
            num_scalar_prefetch=2,
            grid=(f // tf, p // tm),
            in_specs=[
                pl.BlockSpec((tm, d), xmap),
                pl.BlockSpec((None, d, tf), wmap),
                pl.BlockSpec((None, d, tf), wmap),
            ],
            out_specs=pl.BlockSpec((tm, tf), lambda j, i, te, nt: (i, j)),
            scratch_shapes=[pltpu.VMEM((2, d, tf), BF16)],
        ),
        out_shape=jax.ShapeDtypeStruct((p, f), BF16),
        compiler_params=_params(
            2 * tm * d * 2 + 4 * d * tf * 4 + 2 * d * tf * 2 + 2 * tm * tf * 2 + 4 * tm * tf * 4, 2),
        name="grouped_gateup",
    )(tile_expert, n_tiles, xs, wg, wu)


def _pack_pair(lo, hi):
    bl = lax.bitcast_convert_type(lo.astype(BF16).astype(F32), U32) >> 16
    bh = lax.bitcast_convert_type(hi.astype(BF16).astype(F32), U32) & jnp.uint32(0xFFFF0000)
    return bh | bl


def _unpack_pair(w):
    lo = lax.bitcast_convert_type(w << 16, F32)
    hi = lax.bitcast_convert_type(w & jnp.uint32(0xFFFF0000), F32)
    return lo, hi


def _down_kernel(te_ref, nt_ref, a_ref, wlo_ref, whi_ref, o_ref, wb_ref):
    i = pl.program_id(1)
    new_w = (i == 0) | (te_ref[i] != te_ref[jnp.maximum(i - 1, 0)])

    @pl.when(new_w)
    def _():
        wb_ref[0] = wlo_ref[...].astype(BF16)
        wb_ref[1] = whi_ref[...].astype(BF16)

    @pl.when(i < nt_ref[0])
    def _():
        a = a_ref[...]
        o_ref[...] = _pack_pair(_dot(a, wb_ref[0]), _dot(a, wb_ref[1]))

    @pl.when(i >= nt_ref[0])
    def _():
        o_ref[...] = jnp.zeros(o_ref.shape, o_ref.dtype)


def _grouped_down(act, wd, tile_expert, n_tiles, tm, tn=256):
    p, f = act.shape
    n = wd.shape[2]
    half_blocks = (n // 2) // tn
    return pl.pallas_call(
        _down_kernel,
        grid_spec=pltpu.PrefetchScalarGridSpec(
            num_scalar_prefetch=2,
            grid=(half_blocks, p // tm),
            in_specs=[
                pl.BlockSpec((tm, f), lambda j, i, te, nt: (jnp.minimum(i, nt[0] - 1), 0)),
                pl.BlockSpec((None, f, tn), lambda j, i, te, nt: (te[i], 0, j)),
                pl.BlockSpec((None, f, tn), lambda j, i, te, nt: (te[i], 0, half_blocks + j)),
            ],
            out_specs=pl.BlockSpec((tm, tn), lambda j, i, te, nt: (i, j)),
            scratch_shapes=[pltpu.VMEM((2, f, tn), BF16)],
        ),
        out_shape=jax.ShapeDtypeStruct((p, n // 2), U32),
        compiler_params=_params(
            2 * tm * f * 2 + 4 * f * tn * 4 + 2 * f * tn * 2 + 8 * tm * tn * 4, 2),
        name="grouped_down",
    )(tile_expert, n_tiles, act, wd, wd)


def _dense_ffn(x, h, wg, wu, wd, widx, mod, tm_up=1024, tm_down=512):
    t = h.shape[0]
    nt = t // tm_up
    te = jnp.full((nt,), widx, I32)
    act = _grouped_gateup(h, wg, wu, te, jnp.full((1,), nt, I32), tm_up)
    return _proj_residual(act, wd, widx, x, mod, 5, tm=tm_down, tn=512)


def _router_kernel(x_ref, gn_ref, sc_ref, sh_ref, r_ref, h_ref, info_ref, cnt_ref, run_ref, *, n_exp):
    i = pl.program_id(0)
    tm = x_ref.shape[0]

    @pl.when(i == 0)
    def _():
        run_ref[...] = jnp.zeros(run_ref.shape, F32)

    h = _adaln(x_ref[...], gn_ref[...], sc_ref[...], sh_ref[...])
    half = h.shape[1] // 2
    h_ref[...] = _pack_pair(h[:, :half], h[:, half:])
    h_hi = h.astype(BF16)
    h_lo = (h - h_hi.astype(F32)).astype(BF16)
    r = r_ref[...]
    r_hi = r.astype(BF16)
    r_lo = (r - r_hi.astype(F32)).astype(BF16)
    logits = _dot(h_hi, r_hi) + (_dot(h_lo, r_hi) + _dot(h_hi, r_lo))

    lane = lax.broadcasted_iota(I32, (tm, LANES), 1).astype(F32)
    lg = jnp.where(lane < n_exp, logits, -jnp.inf)
    v1 = jnp.max(lg, axis=1, keepdims=True)
    i1 = jnp.min(jnp.where(lg == v1, lane, float(LANES)), axis=1, keepdims=True)
    lg2 = jnp.where(lane == i1, -jnp.inf, lg)
    v2 = jnp.max(lg2, axis=1, keepdims=True)
    i2 = jnp.min(jnp.where(lg2 == v2, lane, float(LANES)), axis=1, keepdims=True)
    e = jnp.exp(v2 - v1)
    g1 = 1.0 / (1.0 + e)
    g2 = e / (1.0 + e)

    oh = jnp.where((lane == i1) | (lane == i2), 1.0, 0.0)
    tri = jnp.where(lax.broadcasted_iota(I32, (tm, tm), 0) > lax.broadcasted_iota(I32, (tm, tm), 1),
                    1.0, 0.0).astype(BF16)
    cum = _dot(tri, oh.astype(BF16)) + run_ref[...]
    rank1 = jnp.sum(jnp.where(lane == i1, cum, 0.0), axis=1, keepdims=True)
    rank2 = jnp.sum(jnp.where(lane == i2, cum, 0.0), axis=1, keepdims=True)
    run = run_ref[...] + jnp.sum(oh, axis=0, keepdims=True)
    run_ref[...] = run
    cnt_ref[...] = jnp.broadcast_to(run, cnt_ref.shape)

    info = jnp.zeros((tm, LANES), F32)
    for k, val in enumerate((i1, i2, g1, g2, rank1, rank2)):
        info = jnp.where(lane == float(k), val, info)
    info_ref[...] = info


def _route(x, gain, mod, router, tm=256):
    t, d = x.shape
    n_exp = router.shape[1]
    assert n_exp <= LANES
    rpad = jnp.pad(router, ((0, 0), (0, LANES - n_exp)))
    row = lambda p: pl.BlockSpec((None, 1, d), lambda i: (mod.row(p, i * tm), 0, 0))
    return pl.pallas_call(
        functools.partial(_router_kernel, n_exp=n_exp),
        grid=(t // tm,),
        in_specs=[
            pl.BlockSpec((tm, d), lambda i: (i, 0)),
            pl.BlockSpec((1, d), lambda i: (0, 0)),
            row(4), row(3),
            pl.BlockSpec((d, LANES), lambda i: (0, 0)),
        ],
        out_specs=[
            pl.BlockSpec((tm, d // 2), lambda i: (i, 0)),
            pl.BlockSpec((tm, LANES), lambda i: (i, 0)),
            pl.BlockSpec((8, LANES), lambda i: (0, 0)),
        ],
        out_shape=[
            jax.ShapeDtypeStruct((t, d // 2), U32),
            jax.ShapeDtypeStruct((t, LANES), F32),
            jax.ShapeDtypeStruct((8, LANES), F32),
        ],
        scratch_shapes=[pltpu.VMEM((1, LANES), F32)],
        compiler_params=_params(4 * tm * d * 4 + 4 * tm * d * 4 + 2 * d * LANES * 4, 1),
        name="moe_route",
    )(x, gain.reshape(1, d), mod.table, mod.table, rpad)


ROW_DMA_UNROLL = 8


def _gather_kernel(tok_ref, tot_ref, h_hbm, o_ref, buf_ref, sem):
    i = pl.program_id(0)
    tg, half = buf_ref.shape[1], buf_ref.shape[2]

    def row_copy(slot, r, src_row):
        return pltpu.make_async_copy(
            h_hbm.at[pl.ds(src_row, 1)], buf_ref.at[slot, pl.ds(r, 1)], sem.at[slot])

    def fetch(tile):
        base = tile * tg

        @pl.when(base < tot_ref[0])
        def _():
            def start(r, c):
                row_copy(tile % 2, r, tok_ref[base + r]).start()
                return c

            lax.fori_loop(0, tg, start, 0, unroll=ROW_DMA_UNROLL)

    @pl.when(i == 0)
    def _():
        fetch(i)

    @pl.when(i + 1 < pl.num_programs(0))
    def _():
        fetch(i + 1)

    @pl.when(i * tg < tot_ref[0])
    def _():
        def wait(r, c):
            row_copy(i % 2, r, 0).wait()
            return c

        lax.fori_loop(0, tg, wait, 0, unroll=ROW_DMA_UNROLL)
        lo, hi = _unpack_pair(buf_ref[i % 2])
        o_ref[:, :half] = lo.astype(o_ref.dtype)
        o_ref[:, half:] = hi.astype(o_ref.dtype)

    @pl.when(i * tg >= tot_ref[0])
    def _():
        o_ref[...] = jnp.zeros(o_ref.shape, o_ref.dtype)


def _gather_rows(hp, sorted_tok, total_rows, tg=256):
    half = hp.shape[1]
    p = sorted_tok.shape[0]
    return pl.pallas_call(
        _gather_kernel,
        grid_spec=pltpu.PrefetchScalarGridSpec(
            num_scalar_prefetch=2,
            grid=(p // tg,),
            in_specs=[pl.BlockSpec(memory_space=pl.ANY)],
            out_specs=pl.BlockSpec((tg, 2 * half), lambda i, tok, tot: (i, 0)),
            scratch_shapes=[pltpu.VMEM((2, tg, half), U32), pltpu.SemaphoreType.DMA((2,))],
        ),
        out_shape=jax.ShapeDtypeStruct((p, 2 * half), BF16),
        compiler_params=_params(2 * tg * half * 4 + 2 * tg * half * 4 + 4 * tg * half * 4, 1),
        name="moe_gather",
    )(sorted_tok, total_rows, hp)


def _combine_kernel(*refs, final):
    if final:
        pos_ref, y_hbm, x_ref, info_ref, gf_ref, nf_ref, o_ref, buf_ref, sem = refs
    else:
        pos_ref, y_hbm, x_ref, info_ref, gf_ref, o_ref, buf_ref, sem = refs
    i = pl.program_id(0)
    tc, half = buf_ref.shape[2], buf_ref.shape[3]

    def row_copy(slot, k, r, src_row):
        return pltpu.make_async_copy(
            y_hbm.at[pl.ds(src_row, 1)], buf_ref.at[slot, k, pl.ds(r, 1)], sem.at[slot])

    def fetch(tile):
        base = tile * tc

        def start(r, c):
            for k in range(MOE_TOPK):
                row_copy(tile % 2, k, r, pos_ref[MOE_TOPK * (base + r) + k]).start()
            return c

        lax.fori_loop(0, tc, start, 0, unroll=ROW_DMA_UNROLL)

    @pl.when(i == 0)
    def _():
        fetch(i)

    @pl.when(i + 1 < pl.num_programs(0))
    def _():
        fetch(i + 1)

    def wait(r, c):
        for k in range(MOE_TOPK):
            row_copy(i % 2, k, r, 0).wait()
        return c

    lax.fori_loop(0, tc, wait, 0, unroll=ROW_DMA_UNROLL)
    info = info_ref[...]
    y_lo = jnp.zeros((tc, half), F32)
    y_hi = jnp.zeros((tc, half), F32)
    for k in range(MOE_TOPK):
        lo, hi = _unpack_pair(buf_ref[i % 2, k])
        gate = info[:, MOE_TOPK + k:MOE_TOPK + k + 1]
        y_lo = y_lo + gate * lo
        y_hi = y_hi + gate * hi
    xn_lo = x_ref[:, :half] + gf_ref[:, :half] * y_lo
    xn_hi = x_ref[:, half:] + gf_ref[:, half:] * y_hi
    if final:
        ssq = jnp.sum(xn_lo * xn_lo, axis=-1, keepdims=True) + jnp.sum(xn_hi * xn_hi, axis=-1, keepdims=True)
        inv = lax.rsqrt(ssq / (2 * half) + NORM_EPS)
        xn_lo = xn_lo * inv * nf_ref[:, :half]
        xn_hi = xn_hi * inv * nf_ref[:, half:]
    o_ref[:, :half] = xn_lo
    o_ref[:, half:] = xn_hi


def _combine(y, pos_flat, x, info, mod, final_gain=None, tc=256):
    t, d = x.shape
    final = final_gain is not None
    in_specs = [
        pl.BlockSpec(memory_space=pl.ANY),
        pl.BlockSpec((tc, d), lambda i, pos: (i, 0)),
        pl.BlockSpec((tc, LANES), lambda i, pos: (i, 0)),
        pl.BlockSpec((None, 1, d), lambda i, pos: (mod.row(5, i * tc), 0, 0)),
    ]
    args = [pos_flat, y, x, info, mod.table]
    if final:
        in_specs.append(pl.BlockSpec((1, d), lambda i, pos: (0, 0)))
        args.append(final_gain.reshape(1, d))
    return pl.pallas_call(
        functools.partial(_combine_kernel, final=final),
        grid_spec=pltpu.PrefetchScalarGridSpec(
            num_scalar_prefetch=1,
            grid=(t // tc,),
            in_specs=in_specs,
            out_specs=pl.BlockSpec((tc, d), lambda i, pos: (i, 0)),
            scratch_shapes=[pltpu.VMEM((2, MOE_TOPK, tc, d // 2), U32),
                            pltpu.SemaphoreType.DMA((2,))],
        ),
        out_shape=jax.ShapeDtypeStruct((t, d), F32),
        compiler_params=_params(2 * tc * d * 4 + 4 * tc * d * 4 + 4 * tc * d * 4, 1),
        name="moe_combine",
    )(*args)


def _moe_ffn(x, gain, mod, router, wg, wu, wd, widx, final_gain=None, tm=512):
    t, d = x.shape
    n_exp = router.shape[1]
    wg, wu, wd = (w.reshape((-1,) + w.shape[2:]) for w in (wg, wu, wd))
    hf, info, cnt = _route(x, gain, mod, router)

    idx = info[:, 0:2].astype(I32)
    rank = info[:, 4:6].astype(I32)
    counts = cnt[0, :n_exp].astype(I32)
    padded = ((counts + tm - 1) // tm) * tm
    ends = jnp.cumsum(padded)
    starts = ends - padded
    pos = starts[idx] + rank
    p_max = MOE_TOPK * t + n_exp * tm
    tok = jnp.repeat(jnp.arange(t, dtype=I32), MOE_TOPK)
    sorted_tok = jnp.zeros((p_max,), I32).at[pos.reshape(-1)].set(tok)
    total = ends[-1:].astype(I32)
    n_tiles = total // tm
    tile_row = jnp.minimum(jnp.arange(p_max // tm, dtype=I32), n_tiles[0] - 1) * tm
    tile_expert = widx * n_exp + jnp.minimum(
        jnp.searchsorted(ends, tile_row, side="right"), n_exp - 1).astype(I32)

    xs = _gather_rows(hf, sorted_tok, total)
    act = _grouped_gateup(xs, wg, wu, tile_expert, n_tiles, tm)
    y = _grouped_down(act, wd, tile_expert, n_tiles, tm)
    return _combine(y, pos.reshape(-1), x, info, mod, final_gain)


def kernel(x, c, mod_w, mod_b, norm_mix, norm_ffn, norm_final, moba_wqkv, moba_wo, sc_w_in, sc_conv,
           sc_w_out, pool_w, pool_scale, cf_w1, cf_b1, cf_dw, cf_dw_b, cf_ln_g, cf_ln_b, cf_w2, cf_b2,
           ffn_wg, ffn_wu, ffn_wd, moe_router, moe_wg, moe_wu, moe_wd):
    batch, seq, d = x.shape
    depth = mod_w.shape[0]
    table = _modulation(c, mod_w, mod_b)
    xt = x.reshape(batch * seq, d)
    for layer in range(depth):
        mod = _Mod(table, layer, batch, seq)
        m, j = layer % 4, layer // 4
        if m == 0:
            h = _prenorm(xt, norm_mix[layer], mod, 0)
            qkv = _qkv_rope(h, moba_wqkv[j], seq)
            o = _moba_attention(qkv, batch, seq, d)
            xt = _proj_residual(o, moba_wo, j, xt, mod, 2)
        elif m == 1:
            h = _prenorm(xt, norm_mix[layer], mod, 0)
            v = _short_conv_in(h, sc_w_in[j], sc_conv[j], seq)
            xt = _proj_residual(v, sc_w_out, j, xt, mod, 2)
        elif m == 2:
            xt = _pool_mixer(xt, norm_mix[layer], mod, pool_w[j], pool_scale[j], seq)
        else:
            h = _prenorm(xt, norm_mix[layer], mod, 0)
            u = _conformer_glu(h, cf_w1[j], cf_b1[j])
            z = _conformer_conv(u, cf_dw[j], cf_dw_b[j], cf_ln_g[j], cf_ln_b[j], seq)
            xt = _proj_residual(z, cf_w2, j, xt, mod, 2, bias=cf_b2[j])

        i = layer // 2
        if layer % 2 == 0:
            h = _prenorm(xt, norm_ffn[layer], mod, 3)
            xt = _dense_ffn(xt, h, ffn_wg, ffn_wu, ffn_wd, i, mod)
        else:
            last = layer == depth - 1
            xt = _moe_ffn(xt, norm_ffn[layer], mod, moe_router[i], moe_wg, moe_wu, moe_wd, i,
                          final_gain=norm_final if last else None)
    if depth % 2 == 1:
        xt = _final_norm(xt, norm_final)
    return xt.reshape(batch, seq, d)
```

```python
import functools

import jax
import jax.numpy as jnp
from jax import lax
from jax.experimental import pallas as pl
from jax.experimental.pallas import tpu as pltpu

F32 = jnp.float32
BF16 = jnp.bfloat16
I32 = jnp.int32
U32 = jnp.uint32

HEAD_DIM = 128
MOBA_BLOCK = 256
MOBA_TOPK = 3
ROPE_THETA = 10000.0
POOL_WINDOWS = (2, 4, 8, 16)
MOE_TOPK = 2
NORM_EPS = 1e-6
LN_EPS = 1e-5
NEG_INF = -1e30

LANES = 128
SUBLANES = 8
VMEM_LIMIT_CAP = 60000 * 1024
MOD_PARTS = 6
STAGE_SLAB_BYTES = 12 << 20
CAST_ROWS = 256


def _params(vmem_bytes, n_axes):
    limit = min(int(vmem_bytes) + (6 << 20), VMEM_LIMIT_CAP)
    return pltpu.CompilerParams(
        dimension_semantics=("arbitrary",) * n_axes, vmem_limit_bytes=limit)


def _dot(a, b):
    return jnp.dot(a, b, preferred_element_type=F32)


def _dot_nt(a, b):
    return lax.dot_general(a, b, (((1,), (1,)), ((), ())), preferred_element_type=F32)


def _silu(x):
    return x * jax.nn.sigmoid(x)


def _adaln(x, g, sc, sh):
    ms = jnp.mean(x * x, axis=-1, keepdims=True)
    return (x * lax.rsqrt(ms + NORM_EPS)) * g * (1.0 + sc) + sh


def _mod_kernel(c_ref, w_ref, b_ref, o_ref):
    c = c_ref[...]
    ca = _silu(c).astype(BF16)
    o_ref[...] = _dot(ca, w_ref[...].astype(BF16)) + b_ref[...]


def _modulation(c, mod_w, mod_b):
    depth, d, n = mod_w.shape
    b = c.shape[0]
    tn = 1024
    out = pl.pallas_call(
        _mod_kernel,
        grid=(depth, n // tn),
        in_specs=[
            pl.BlockSpec((b, d), lambda l, j: (0, 0)),
            pl.BlockSpec((None, d, tn), lambda l, j: (l, 0, j)),
            pl.BlockSpec((None, 1, tn), lambda l, j: (l, 0, j)),
        ],
        out_specs=pl.BlockSpec((None, b, tn), lambda l, j: (l, 0, j)),
        out_shape=jax.ShapeDtypeStruct((depth, b, n), F32),
        compiler_params=_params(2 * d * tn * 4 + d * tn * 2, 2),
        name="modulation",
    )(c, mod_w, mod_b.reshape(depth, 1, n))
    return out.reshape(depth * b * MOD_PARTS, 1, d)


class _Mod:
    def __init__(self, table, layer, batch, seq):
        self.table = table
        self.base = layer * batch * MOD_PARTS
        self.seq = seq

    def row(self, part, tok0):
        return self.base + (tok0 // self.seq) * MOD_PARTS + part


def _prenorm_kernel(x_ref, g_ref, sc_ref, sh_ref, o_ref):
    o_ref[...] = _adaln(x_ref[...], g_ref[...], sc_ref[...], sh_ref[...]).astype(o_ref.dtype)


def _prenorm(x, gain, mod, part0, tm=512):
    t, d = x.shape
    row = lambda p: pl.BlockSpec((None, 1, d), lambda i: (mod.row(p, i * tm), 0, 0))
    return pl.pallas_call(
        _prenorm_kernel,
        grid=(t // tm,),
        in_specs=[
            pl.BlockSpec((tm, d), lambda i: (i, 0)),
            pl.BlockSpec((1, d), lambda i: (0, 0)),
            row(part0 + 1), row(part0),
        ],
        out_specs=pl.BlockSpec((tm, d), lambda i: (i, 0)),
        out_shape=jax.ShapeDtypeStruct((t, d), BF16),
        compiler_params=_params(2 * tm * d * 6 + 3 * tm * d * 4, 1),
        name="prenorm",
    )(x, gain.reshape(1, d), mod.table, mod.table)


def _final_norm_kernel(x_ref, g_ref, o_ref):
    x = x_ref[...]
    ms = jnp.mean(x * x, axis=-1, keepdims=True)
    o_ref[...] = x * lax.rsqrt(ms + NORM_EPS) * g_ref[...]


def _final_norm(x, gain, tm=512):
    t, d = x.shape
    return pl.pallas_call(
        _final_norm_kernel,
        grid=(t // tm,),
        in_specs=[pl.BlockSpec((tm, d), lambda i: (i, 0)), pl.BlockSpec((1, d), lambda i: (0, 0))],
        out_specs=pl.BlockSpec((tm, d), lambda i: (i, 0)),
        out_shape=jax.ShapeDtypeStruct((t, d), F32),
        compiler_params=_params(6 * tm * d * 4, 1),
        name="final_norm",
    )(x, gain.reshape(1, d))


def _qkv_kernel(x_ref, w_ref, cos_ref, sin_ref, o_ref, wb_ref, *, nq, scale):
    j = pl.program_id(0)

    @pl.when(pl.program_id(1) == 0)
    def _():
        wb_ref[...] = w_ref[...].astype(BF16)

    acc = _dot(x_ref[...], wb_ref[...])
    tn = acc.shape[1]

    def rope(mult):
        cos = cos_ref[...]
        sin = sin_ref[...]
        for c in range(tn // HEAD_DIM):
            a = acc[:, c * HEAD_DIM:(c + 1) * HEAD_DIM]
            r = a * cos + pltpu.roll(a, HEAD_DIM // 2, 1) * sin
            if mult != 1.0:
                r = r * mult
            o_ref[:, c * HEAD_DIM:(c + 1) * HEAD_DIM] = r.astype(o_ref.dtype)

    @pl.when(j < nq)
    def _():
        rope(scale)

    @pl.when((j >= nq) & (j < 2 * nq))
    def _():
        rope(1.0)

    @pl.when(j >= 2 * nq)
    def _():
        o_ref[...] = acc.astype(o_ref.dtype)


def _qkv_rope(h, w, seq, tm=1024, tn=1024):
    t, d = h.shape
    n = w.shape[1]
    tn = min(tn, d)
    half = HEAD_DIM // 2
    inv = ROPE_THETA ** (-jnp.arange(half, dtype=F32) * (2.0 / HEAD_DIM))
    ang = jnp.arange(seq, dtype=F32)[:, None] * inv[None, :]
    cos = jnp.concatenate([jnp.cos(ang), jnp.cos(ang)], axis=-1)
    sin = jnp.concatenate([-jnp.sin(ang), jnp.sin(ang)], axis=-1)
    tm = min(tm, seq)
    sblk = seq // tm
    kern = functools.partial(_qkv_kernel, nq=d // tn, scale=HEAD_DIM ** -0.5)
    return pl.pallas_call(
        kern,
        grid=(n // tn, t // tm),
        in_specs=[
            pl.BlockSpec((tm, d), lambda j, i: (i, 0)),
            pl.BlockSpec((d, tn), lambda j, i: (0, j)),
            pl.BlockSpec((tm, HEAD_DIM), lambda j, i: (i % sblk, 0)),
            pl.BlockSpec((tm, HEAD_DIM), lambda j, i: (i % sblk, 0)),
        ],
        out_specs=pl.BlockSpec((tm, tn), lambda j, i: (i, j)),
        out_shape=jax.ShapeDtypeStruct((t, n), BF16),
        scratch_shapes=[pltpu.VMEM((d, tn), BF16)],
        compiler_params=_params(
            2 * tm * d * 2 + 2 * d * tn * 4 + d * tn * 2 + 2 * tm * tn * 2 + 3 * tm * tn * 4, 2),
        name="qkv_rope",
    )(h, w, cos, sin)


def _moba_kernel(q_ref, k_ref, v_ref, o_ref, kbar_ref, vt_ref, s_ref, p_ref, *, nb, blk, topk):
    kr = kbar_ref.shape[0]

    kbar_ref[...] = jnp.zeros(kbar_ref.shape, F32)
    for m in range(nb):
        rows = slice(m * blk, (m + 1) * blk)
        kbar_ref[m:m + 1, :] = jnp.mean(k_ref[rows, :].astype(F32), axis=0, keepdims=True)
        vt_ref[:, rows] = v_ref[rows, :].astype(F32).T.astype(BF16)
    kbar = kbar_ref[...]
    kb_hi = kbar.astype(BF16)
    kb_lo = (kbar - kb_hi.astype(F32)).astype(BF16)

    blk_id = lax.broadcasted_iota(I32, (kr, blk), 0)
    causal = lax.broadcasted_iota(I32, (blk, blk), 0) <= lax.broadcasted_iota(I32, (blk, blk), 1)

    for n in range(nb):
        q = q_ref[n * blk:(n + 1) * blk, :]
        if n > 0:
            gate = _dot_nt(kb_hi, q) + _dot_nt(kb_lo, q)
            valid = blk_id < n
            g = jnp.where(valid, gate, NEG_INF)
            cnt = jnp.zeros((kr, blk), F32)
            for mp in range(n):
                other = g[mp:mp + 1, :]
                cnt = cnt + jnp.where(other > g, 1.0, jnp.where((other == g) & (blk_id > mp), 1.0, 0.0))
            sel = jnp.where(valid & (cnt < topk), 1.0, 0.0)

        mx = None
        for m in range(n + 1):
            rows = slice(m * blk, (m + 1) * blk)
            s = _dot_nt(k_ref[rows, :], q)
            s = jnp.where(causal if m == n else sel[m:m + 1, :] > 0.0, s, NEG_INF)
            s_ref[rows, :] = s
            bm = jnp.max(s, axis=0, keepdims=True)
            mx = bm if mx is None else jnp.maximum(mx, bm)
        denom = jnp.zeros((1, blk), F32)
        for m in range(n + 1):
            rows = slice(m * blk, (m + 1) * blk)
            p = jnp.exp(s_ref[rows, :] - mx)
            denom = denom + jnp.sum(p, axis=0, keepdims=True)
            p_ref[rows, :] = p.astype(BF16)
        keys = (n + 1) * blk
        acc = _dot(vt_ref[:, 0:keys], p_ref[0:keys, :])
        o_ref[n * blk:(n + 1) * blk, :] = (acc / denom).T.astype(o_ref.dtype)


def _moba_attention(qkv, batch, seq, d):
    t = qkv.shape[0]
    heads = d // HEAD_DIM
    blk = MOBA_BLOCK
    assert seq % blk == 0
    nb = seq // blk
    kr = 16
    assert nb <= kr
    kern = functools.partial(_moba_kernel, nb=nb, blk=blk, topk=min(MOBA_TOPK, nb - 1))
    return pl.pallas_call(
        kern,
        grid=(batch, heads),
        in_specs=[
            pl.BlockSpec((seq, HEAD_DIM), lambda b, h: (b, h)),
            pl.BlockSpec((seq, HEAD_DIM), lambda b, h: (b, heads + h)),
            pl.BlockSpec((seq, HEAD_DIM), lambda b, h: (b, 2 * heads + h)),
        ],
        out_specs=pl.BlockSpec((seq, HEAD_DIM), lambda b, h: (b, h)),
        out_shape=jax.ShapeDtypeStruct((t, d), BF16),
        scratch_shapes=[pltpu.VMEM((kr, HEAD_DIM), F32), pltpu.VMEM((HEAD_DIM, seq), BF16),
                        pltpu.VMEM((seq, blk), F32), pltpu.VMEM((seq, blk), BF16)],
        compiler_params=_params(10 * seq * HEAD_DIM * 2 + seq * blk * 6 + 16 * blk * blk * 4, 2),
        name="moba_attention",
    )(qkv, qkv, qkv)


def _proj_res_kernel(*refs, has_bias):
    if has_bias:
        a_ref, w_ref, r_ref, g_ref, b_ref, o_ref, wb_ref = refs
    else:
        a_ref, w_ref, r_ref, g_ref, o_ref, wb_ref = refs

    @pl.when(pl.program_id(1) == 0)
    def _():
        wb_ref[...] = w_ref[...].astype(BF16)

    y = _dot(a_ref[...], wb_ref[...])
    if has_bias:
        y = y + b_ref[...]
    o_ref[...] = r_ref[...] + g_ref[...] * y


def _proj_residual(a, w3, widx, resid, mod, gate_part, bias=None, tm=1024, tn=512):
    t, k = a.shape
    n = w3.shape[2]
    in_specs = [
        pl.BlockSpec((tm, k), lambda j, i: (i, 0)),
        pl.BlockSpec((None, k, tn), lambda j, i: (widx, 0, j)),
        pl.BlockSpec((tm, tn), lambda j, i: (i, j)),
        pl.BlockSpec((None, 1, tn), lambda j, i: (mod.row(gate_part, i * tm), 0, j)),
    ]
    args = [a, w3, resid, mod.table]
    if bias is not None:
        in_specs.append(pl.BlockSpec((1, tn), lambda j, i: (0, j)))
        args.append(bias.reshape(1, n))
    return pl.pallas_call(
        functools.partial(_proj_res_kernel, has_bias=bias is not None),
        grid=(n // tn, t // tm),
        in_specs=in_specs,
        out_specs=pl.BlockSpec((tm, tn), lambda j, i: (i, j)),
        out_shape=jax.ShapeDtypeStruct((t, n), F32),
        scratch_shapes=[pltpu.VMEM((k, tn), BF16)],
        compiler_params=_params(
            2 * tm * k * 2 + 2 * k * tn * 4 + k * tn * 2 + 5 * tm * tn * 4, 2),
        name="proj_residual",
    )(*args)


def _shift_rows(p, halo, k):
    hr = halo.shape[0]
    body = pltpu.roll(p, k, 0)
    head = jnp.where(lax.broadcasted_iota(I32, (hr, p.shape[1]), 0) < k,
                     pltpu.roll(halo, k, 0), body[:hr])
    return head, body


def _shortconv_kernel(x_ref, wb_g, wc_g, wx_g, cw_ref, o_ref, wb_ref, halo_ref, *, width, tiles_per_seq):
    i = pl.program_id(1)

    @pl.when(i == 0)
    def _():
        wb_ref[0] = wb_g[...].astype(BF16)
        wb_ref[1] = wc_g[...].astype(BF16)
        wb_ref[2] = wx_g[...].astype(BF16)

    @pl.when(i % tiles_per_seq == 0)
    def _():
        halo_ref[...] = jnp.zeros(halo_ref.shape, F32)

    x = x_ref[...]
    gb = _dot(x, wb_ref[0])
    p = _dot(x, wb_ref[1]) * _dot(x, wb_ref[2])
    hr = halo_ref.shape[0]
    halo = halo_ref[...]
    cw = cw_ref[...]
    body = cw[width - 1:width, :] * p
    head = body[:hr]
    for s in range(1, width):
        hs, bs = _shift_rows(p, halo, s)
        w = cw[width - 1 - s:width - s, :]
        body = body + w * bs
        head = head + w * hs
    o_ref[...] = (gb * body).astype(o_ref.dtype)
    o_ref[0:hr, :] = (gb[:hr] * head).astype(o_ref.dtype)
    halo_ref[...] = p[p.shape[0] - hr:, :]


def _short_conv_in(h, w_in, conv_w, seq, tm=1024, tn=256):
    t, d = h.shape
    width = conv_w.shape[0]
    nd = d // tn
    hr = 16
    assert width - 1 <= hr
    kern = functools.partial(_shortconv_kernel, width=width, tiles_per_seq=seq // tm)
    wspec = lambda off: pl.BlockSpec((d, tn), lambda j, i: (0, off * nd + j))
    return pl.pallas_call(
        kern,
        grid=(nd, t // tm),
        in_specs=[
            pl.BlockSpec((tm, d), lambda j, i: (i, 0)),
            wspec(0), wspec(1), wspec(2),
            pl.BlockSpec((width, tn), lambda j, i: (0, j)),
        ],
        out_specs=pl.BlockSpec((tm, tn), lambda j, i: (i, j)),
        out_shape=jax.ShapeDtypeStruct((t, d), BF16),
        scratch_shapes=[pltpu.VMEM((3, d, tn), BF16), pltpu.VMEM((hr, tn), F32)],
        compiler_params=_params(
            2 * tm * d * 2 + 6 * d * tn * 4 + 3 * d * tn * 2 + 10 * tm * tn * 4, 2),
        name="short_conv_in",
    )(h, w_in, w_in, w_in, conv_w)


def _pool_kernel(x_ref, xh_ref, gn_ref, sc_ref, sh_ref, w_ref, ps_ref, ga_ref, o_ref,
                 wb_ref, ext_ref, *, windows, tiles_per_seq):
    i = pl.program_id(0)
    tm, d = x_ref.shape
    hr = xh_ref.shape[0]
    cg = d // len(windows)

    @pl.when(i == 0)
    def _():
        wb_ref[...] = w_ref[...].astype(BF16)

    x = x_ref[...]
    gn, sc, sh = gn_ref[...], sc_ref[...], sh_ref[...]
    h = _adaln(x, gn, sc, sh)
    hh = _adaln(xh_ref[...], gn, sc, sh)
    first = i % tiles_per_seq == 0
    ext_ref[0:hr, :] = jnp.where(first, 0.0, hh)
    ext_ref[hr:, :] = h
    pos = (i % tiles_per_seq) * tm + lax.broadcasted_iota(I32, (tm, 1), 0)
    for g, w in enumerate(windows):
        cols = slice(g * cg, (g + 1) * cg)
        s = ext_ref[:, cols]
        step = 1
        while step < w:
            s = s + pltpu.roll(s, step, 0)
            step *= 2
        count = jnp.minimum(pos + 1, w).astype(F32)
        pooled = s[hr:] / count - h[:, cols]
        mixed = _dot(pooled.astype(BF16), wb_ref[g])
        o_ref[:, cols] = x[:, cols] + ga_ref[:, cols] * (mixed * ps_ref[:, cols])


def _pool_mixer(x, gain, mod, w_group, scale, seq, tm=256):
    t, d = x.shape
    g, cg, _ = w_group.shape
    assert g == len(POOL_WINDOWS) and cg % LANES == 0
    hr = 16
    assert max(POOL_WINDOWS) <= hr and all(w & (w - 1) == 0 for w in POOL_WINDOWS)
    kern = functools.partial(_pool_kernel, windows=POOL_WINDOWS, tiles_per_seq=seq // tm)
    row = lambda p: pl.BlockSpec((None, 1, d), lambda i: (mod.row(p, i * tm), 0, 0))
    return pl.pallas_call(
        kern,
        grid=(t // tm,),
        in_specs=[
            pl.BlockSpec((tm, d), lambda i: (i, 0)),
            pl.BlockSpec((hr, d), lambda i: (jnp.maximum(i * (tm // hr) - 1, 0), 0)),
            pl.BlockSpec((1, d), lambda i: (0, 0)),
            row(1), row(0),
            pl.BlockSpec((g, cg, cg), lambda i: (0, 0, 0)),
            pl.BlockSpec((1, d), lambda i: (0, 0)),
            row(2),
        ],
        out_specs=pl.BlockSpec((tm, d), lambda i: (i, 0)),
        out_shape=jax.ShapeDtypeStruct((t, d), F32),
        scratch_shapes=[pltpu.VMEM((g, cg, cg), BF16), pltpu.VMEM((tm + hr, d), F32)],
        compiler_params=_params(4 * tm * d * 4 + 2 * g * cg * cg * 4 + g * cg * cg * 2
                                + 6 * tm * d * 4, 1),
        name="pool_mixer",
    )(x, x, gain.reshape(1, d), mod.table, mod.table, w_group, scale.reshape(1, d), mod.table)


def _glu_kernel(x_ref, wa_ref, wg_ref, ba_ref, bg_ref, o_ref, wb_ref):
    @pl.when(pl.program_id(1) == 0)
    def _():
        wb_ref[0] = wa_ref[...].astype(BF16)
        wb_ref[1] = wg_ref[...].astype(BF16)

    x = x_ref[...]
    a = _dot(x, wb_ref[0]) + ba_ref[...]
    g = _dot(x, wb_ref[1]) + bg_ref[...]
    o_ref[...] = a * jax.nn.sigmoid(g)


def _conformer_glu(h, w1, b1, tm=1024, tn=512):
    t, d = h.shape
    n = w1.shape[1] // 2
    nd = n // tn
    b1 = b1.reshape(1, 2 * n)
    return pl.pallas_call(
        _glu_kernel,
        grid=(nd, t // tm),
        in_specs=[
            pl.BlockSpec((tm, d), lambda j, i: (i, 0)),
            pl.BlockSpec((d, tn), lambda j, i: (0, j)),
            pl.BlockSpec((d, tn), lambda j, i: (0, nd + j)),
            pl.BlockSpec((1, tn), lambda j, i: (0, j)),
            pl.BlockSpec((1, tn), lambda j, i: (0, nd + j)),
        ],
        out_specs=pl.BlockSpec((tm, tn), lambda j, i: (i, j)),
        out_shape=jax.ShapeDtypeStruct((t, n), F32),
        scratch_shapes=[pltpu.VMEM((2, d, tn), BF16)],
        compiler_params=_params(
            2 * tm * d * 2 + 4 * d * tn * 4 + 2 * d * tn * 2 + 6 * tm * tn * 4, 2),
        name="conformer_glu",
    )(h, w1, w1, b1, b1)


def _cfconv_kernel(u_ref, uh_ref, dw_ref, dwb_ref, lg_ref, lb_ref, o_ref, ph_ref, y_ref,
                   *, width, tiles_per_seq, rc, cc):
    i = pl.program_id(0)
    tm, d = u_ref.shape
    hr = uh_ref.shape[0]
    rows = tm + hr
    first = i % tiles_per_seq == 0
    off = hr - (width - 1)
    for c0 in range(0, d, cc):
        cols = slice(c0, c0 + cc)
        ph_ref[0, 0:hr, :] = jnp.where(first, 0.0, uh_ref[:, cols])
        ph_ref[0, hr:, :] = u_ref[:, cols]
        e0 = ph_ref[0]
        for s in range(1, SUBLANES):
            ph_ref[s] = pltpu.roll(e0, rows - s, 0)
        w = dw_ref[:, cols]
        for r0 in range(0, tm, rc):
            acc = jnp.zeros((rc, cc), F32) + dwb_ref[:, cols]
            for k in range(width):
                q, s = divmod(off + k, SUBLANES)
                a0 = r0 + q * SUBLANES
                acc = acc + w[k:k + 1, :] * ph_ref[s, a0:a0 + rc, :]
            y_ref[r0:r0 + rc, cols] = acc
    y = y_ref[...]
    mu = jnp.mean(y, axis=-1, keepdims=True)
    yc = y - mu
    var = jnp.mean(yc * yc, axis=-1, keepdims=True)
    z = yc * lax.rsqrt(var + LN_EPS) * lg_ref[...] + lb_ref[...]
    o_ref[...] = _silu(z).astype(o_ref.dtype)


def _conformer_conv(u, dw, dw_b, ln_g, ln_b, seq, tm=256):
    t, d = u.shape
    width = dw.shape[0]
    hr = 32
    assert width - 1 <= hr
    cc = min(256, d)
    kern = functools.partial(_cfconv_kernel, width=width, tiles_per_seq=seq // tm,
                             rc=min(64, tm), cc=cc)
    vec = lambda: pl.BlockSpec((1, d), lambda i: (0, 0))
    return pl.pallas_call(
        kern,
        grid=(t // tm,),
        in_specs=[
            pl.BlockSpec((tm, d), lambda i: (i, 0)),
            pl.BlockSpec((hr, d), lambda i: (jnp.maximum(i * (tm // hr) - 1, 0), 0)),
            pl.BlockSpec((width, d), lambda i: (0, 0)),
            vec(), vec(), vec(),
        ],
        out_specs=pl.BlockSpec((tm, d), lambda i: (i, 0)),
        out_shape=jax.ShapeDtypeStruct((t, d), BF16),
        scratch_shapes=[pltpu.VMEM((SUBLANES, tm + hr, cc), F32), pltpu.VMEM((tm, d), F32)],
        compiler_params=_params(8 * tm * d * 4 + SUBLANES * (tm + hr) * cc * 4, 1),
        name="conformer_conv",
    )(u, u, dw, dw_b.reshape(1, d), ln_g.reshape(1, d), ln_b.reshape(1, d))


def _pick_cols(n, k):
    best = None
    for c in range(LANES, n + 1, LANES):
        if n % c == 0 and k * c * 4 <= STAGE_SLAB_BYTES:
            best = c
    assert best is not None
    return best


def _stage_weights(te_ref, nt_ref, nxt_ref, slabs, stage_ref, wb_ref, sem):
    j, i = pl.program_id(0), pl.program_id(1)
    k, tn = stage_ref.shape[1], stage_ref.shape[2]

    def copies(e, jj):
        return [pltpu.make_async_copy(
            w.at[e, :, pl.ds(pl.multiple_of(col(jj) * tn, LANES), tn)], stage_ref.at[s], sem.at[s])
            for s, (w, col) in enumerate(slabs)]

    new_w = (i == 0) | (te_ref[i] != te_ref[jnp.maximum(i - 1, 0)])

    @pl.when(new_w)
    def _():
        @pl.when((i == 0) & (j == 0))
        def _():
            for c in copies(te_ref[0], 0):
                c.start()

        for c in copies(te_ref[i], j):
            c.wait()

        def cast(c, carry):
            r = pl.multiple_of(c * CAST_ROWS, CAST_ROWS)
            for s in range(len(slabs)):
                wb_ref[s, pl.ds(r, CAST_ROWS), :] = stage_ref[s, pl.ds(r, CAST_ROWS), :].astype(BF16)
            return carry

        lax.fori_loop(0, k // CAST_ROWS, cast, 0)

        nxt = nxt_ref[i]
        more = nxt < nt_ref[0]

        @pl.when(more)
        def _():
            for c in copies(te_ref[nxt], j):
                c.start()

        @pl.when(jnp.logical_not(more) & (j + 1 < pl.num_programs(0)))
        def _():
            for c in copies(te_ref[0], j + 1):
                c.start()


def _gateup_kernel(te_ref, nt_ref, nxt_ref, x_ref, wg_hbm, wu_hbm, o_ref, stage_ref, wb_ref, sem, *, rows):
    i = pl.program_id(1)
    same = lambda jj: jj
    _stage_weights(te_ref, nt_ref, nxt_ref, [(wg_hbm, same), (wu_hbm, same)], stage_ref, wb_ref, sem)

    @pl.when(i < nt_ref[0])
    def _():
        for r0 in range(0, x_ref.shape[0], rows):
            x = x_ref[r0:r0 + rows, :]
            g = _dot(x, wb_ref[0])
            u = _dot(x, wb_ref[1])
            o_ref[r0:r0 + rows, :] = (_silu(g) * u).astype(o_ref.dtype)

    @pl.when(i >= nt_ref[0])
    def _():
        o_ref[...] = jnp.zeros(o_ref.shape, o_ref.dtype)


def _grouped_gateup(xs, wg, wu, tile_expert, n_tiles, next_group, tm):
    p, d = xs.shape
    f = wg.shape[2]
    assert d % CAST_ROWS == 0
    tf = _pick_cols(f, d)
    rows = min(256, tm)
    return pl.pallas_call(
        functools.partial(_gateup_kernel, rows=rows),
        grid_spec=pltpu.PrefetchScalarGridSpec(
            num_scalar_prefetch=3,
            grid=(f // tf, p // tm),
            in_specs=[
                pl.BlockSpec((tm, d), lambda j, i, te, nt, nx: (jnp.minimum(i, nt[0] - 1), 0)),
                pl.BlockSpec(memory_space=pltpu.HBM),
                pl.BlockSpec(memory_space=pltpu.HBM),
            ],
            out_specs=pl.BlockSpec((tm, tf), lambda j, i, te, nt, nx: (i, j)),
            scratch_shapes=[pltpu.VMEM((2, d, tf), F32), pltpu.VMEM((2, d, tf), BF16),
                            pltpu.SemaphoreType.DMA((2,))],
        ),
        out_shape=jax.ShapeDtypeStruct((p, f), BF16),
        compiler_params=_params(
            2 * d * tf * 6 + 2 * tm * d * 2 + 2 * tm * tf * 2 + 4 * rows * tf * 4, 2),
        name="grouped_gateup",
    )(tile_expert, n_tiles, next_group, xs, wg, wu)


def _pack_pair(lo, hi):
    bl = lax.bitcast_convert_type(lo.astype(BF16).astype(F32), U32) >> 16
    bh = lax.bitcast_convert_type(hi.astype(BF16).astype(F32), U32) & jnp.uint32(0xFFFF0000)
    return bh | bl


def _unpack_pair(w):
    lo = lax.bitcast_convert_type(w << 16, F32)
    hi = lax.bitcast_convert_type(w & jnp.uint32(0xFFFF0000), F32)
    return lo, hi


def _down_kernel(te_ref, nt_ref, nxt_ref, a_ref, w_hbm, o_ref, stage_ref, wb_ref, sem, *, half_blocks):
    i = pl.program_id(1)
    slabs = [(w_hbm, lambda jj: jj), (w_hbm, lambda jj: half_blocks + jj)]
    _stage_weights(te_ref, nt_ref, nxt_ref, slabs, stage_ref, wb_ref, sem)

    @pl.when(i < nt_ref[0])
    def _():
        a = a_ref[...]
        o_ref[...] = _pack_pair(_dot(a, wb_ref[0]), _dot(a, wb_ref[1]))

    @pl.when(i >= nt_ref[0])
    def _():
        o_ref[...] = jnp.zeros(o_ref.shape, o_ref.dtype)


def _grouped_down(act, wd, tile_expert, n_tiles, next_group, tm):
    p, f = act.shape
    n = wd.shape[2]
    assert f % CAST_ROWS == 0
    tn = _pick_cols(n // 2, f)
    half_blocks = (n // 2) // tn
    return pl.pallas_call(
        functools.partial(_down_kernel, half_blocks=half_blocks),
        grid_spec=pltpu.PrefetchScalarGridSpec(
            num_scalar_prefetch=3,
            grid=(half_blocks, p // tm),
            in_specs=[
                pl.BlockSpec((tm, f), lambda j, i, te, nt, nx: (jnp.minimum(i, nt[0] - 1), 0)),
                pl.BlockSpec(memory_space=pltpu.HBM),
            ],
            out_specs=pl.BlockSpec((tm, tn), lambda j, i, te, nt, nx: (i, j)),
            scratch_shapes=[pltpu.VMEM((2, f, tn), F32), pltpu.VMEM((2, f, tn), BF16),
                            pltpu.SemaphoreType.DMA((2,))],
        ),
        out_shape=jax.ShapeDtypeStruct((p, n // 2), U32),
        compiler_params=_params(2 * f * tn * 6 + 2 * tm * f * 2 + 8 * tm * tn * 4, 2),
        name="grouped_down",
    )(tile_expert, n_tiles, next_group, act, wd)


def _dense_ffn(x, h, wg, wu, wd, widx, mod, tm=512):
    t = h.shape[0]
    nt = t // tm
    te = jnp.full((nt,), widx, I32)
    single_group = jnp.full((nt,), nt, I32)
    act = _grouped_gateup(h, wg, wu, te, jnp.full((1,), nt, I32), single_group, tm)
    return _proj_residual(act, wd, widx, x, mod, 5, tm=tm, tn=512)


def _router_kernel(x_ref, gn_ref, sc_ref, sh_ref, r_ref, h_ref, info_ref, cnt_ref, run_ref, *, n_exp):
    i = pl.program_id(0)
    tm = x_ref.shape[0]

    @pl.when(i == 0)
    def _():
        run_ref[...] = jnp.zeros(run_ref.shape, F32)

    h = _adaln(x_ref[...], gn_ref[...], sc_ref[...], sh_ref[...])
    half = h.shape[1] // 2
    h_ref[...] = _pack_pair(h[:, :half], h[:, half:])
    h_hi = h.astype(BF16)
    h_lo = (h - h_hi.astype(F32)).astype(BF16)
    r = r_ref[...]
    r_hi = r.astype(BF16)
    r_lo = (r - r_hi.astype(F32)).astype(BF16)
    logits = _dot(h_hi, r_hi) + (_dot(h_lo, r_hi) + _dot(h_hi, r_lo))

    lane = lax.broadcasted_iota(I32, (tm, LANES), 1).astype(F32)
    lg = jnp.where(lane < n_exp, logits, -jnp.inf)
    v1 = jnp.max(lg, axis=1, keepdims=True)
    i1 = jnp.min(jnp.where(lg == v1, lane, float(LANES)), axis=1, keepdims=True)
    lg2 = jnp.where(lane == i1, -jnp.inf, lg)
    v2 = jnp.max(lg2, axis=1, keepdims=True)
    i2 = jnp.min(jnp.where(lg2 == v2, lane, float(LANES)), axis=1, keepdims=True)
    e = jnp.exp(v2 - v1)
    g1 = 1.0 / (1.0 + e)
    g2 = e / (1.0 + e)

    oh = jnp.where((lane == i1) | (lane == i2), 1.0, 0.0)
    tri = jnp.where(lax.broadcasted_iota(I32, (tm, tm), 0) > lax.broadcasted_iota(I32, (tm, tm), 1),
                    1.0, 0.0).astype(BF16)
    cum = _dot(tri, oh.astype(BF16)) + run_ref[...]
    rank1 = jnp.sum(jnp.where(lane == i1, cum, 0.0), axis=1, keepdims=True)
    rank2 = jnp.sum(jnp.where(lane == i2, cum, 0.0), axis=1, keepdims=True)
    run = run_ref[...] + jnp.sum(oh, axis=0, keepdims=True)
    run_ref[...] = run
    cnt_ref[...] = jnp.broadcast_to(run, cnt_ref.shape)

    info = jnp.zeros((tm, LANES), F32)
    for k, val in enumerate((i1, i2, g1, g2, rank1, rank2)):
        info = jnp.where(lane == float(k), val, info)
    info_ref[...] = info


def _route(x, gain, mod, router, tm=256):
    t, d = x.shape
    n_exp = router.shape[1]
    assert n_exp <= LANES
    rpad = jnp.pad(router, ((0, 0), (0, LANES - n_exp)))
    row = lambda p: pl.BlockSpec((None, 1, d), lambda i: (mod.row(p, i * tm), 0, 0))
    return pl.pallas_call(
        functools.partial(_router_kernel, n_exp=n_exp),
        grid=(t // tm,),
        in_specs=[
            pl.BlockSpec((tm, d), lambda i: (i, 0)),
            pl.BlockSpec((1, d), lambda i: (0, 0)),
            row(4), row(3),
            pl.BlockSpec((d, LANES), lambda i: (0, 0)),
        ],
        out_specs=[
            pl.BlockSpec((tm, d // 2), lambda i: (i, 0)),
            pl.BlockSpec((tm, LANES), lambda i: (i, 0)),
            pl.BlockSpec((8, LANES), lambda i: (0, 0)),
        ],
        out_shape=[
            jax.ShapeDtypeStruct((t, d // 2), U32),
            jax.ShapeDtypeStruct((t, LANES), F32),
            jax.ShapeDtypeStruct((8, LANES), F32),
        ],
        scratch_shapes=[pltpu.VMEM((1, LANES), F32)],
        compiler_params=_params(4 * tm * d * 4 + 4 * tm * d * 4 + 2 * d * LANES * 4, 1),
        name="moe_route",
    )(x, gain.reshape(1, d), mod.table, mod.table, rpad)


ROW_DMA_UNROLL = 8


def _gather_kernel(tok_ref, tot_ref, hp_ref, o_ref, buf_ref):
    i = pl.program_id(0)
    tg, half = buf_ref.shape
    base = i * tg

    @pl.when(base < tot_ref[0])
    def _():
        def copy_row(r, c):
            buf_ref[pl.ds(r, 1), :] = hp_ref[pl.ds(tok_ref[base + r], 1), :]
            return c

        lax.fori_loop(0, tg, copy_row, 0, unroll=ROW_DMA_UNROLL)
        lo, hi = _unpack_pair(buf_ref[...])
        o_ref[:, :half] = lo.astype(o_ref.dtype)
        o_ref[:, half:] = hi.astype(o_ref.dtype)

    @pl.when(base >= tot_ref[0])
    def _():
        o_ref[...] = jnp.zeros(o_ref.shape, o_ref.dtype)


def _gather_rows(hp, sorted_tok, total_rows, tg=256):
    t, half = hp.shape
    p = sorted_tok.shape[0]
    return pl.pallas_call(
        _gather_kernel,
        grid_spec=pltpu.PrefetchScalarGridSpec(
            num_scalar_prefetch=2,
            grid=(p // tg,),
            in_specs=[pl.BlockSpec(memory_space=pltpu.VMEM)],
            out_specs=pl.BlockSpec((tg, 2 * half), lambda i, tok, tot: (i, 0)),
            scratch_shapes=[pltpu.VMEM((tg, half), U32)],
        ),
        out_shape=jax.ShapeDtypeStruct((p, 2 * half), BF16),
        compiler_params=_params(t * half * 4 + 8 * tg * half * 4, 1),
        name="moe_gather",
    )(sorted_tok, total_rows, hp)


def _combine_kernel(*refs, final):
    if final:
        pos_ref, y_hbm, x_ref, info_ref, gf_ref, nf_ref, o_ref, buf_ref, sem = refs
    else:
        pos_ref, y_hbm, x_ref, info_ref, gf_ref, o_ref, buf_ref, sem = refs
    i = pl.program_id(0)
    tc, half = buf_ref.shape[2], buf_ref.shape[3]

    def row_copy(slot, k, r, src_row):
        return pltpu.make_async_copy(
            y_hbm.at[pl.ds(src_row, 1)], buf_ref.at[slot, k, pl.ds(r, 1)], sem.at[slot])

    def fetch(tile):
        base = tile * tc

        def start(r, c):
            for k in range(MOE_TOPK):
                row_copy(tile % 2, k, r, pos_ref[MOE_TOPK * (base + r) + k]).start()
            return c

        lax.fori_loop(0, tc, start, 0, unroll=ROW_DMA_UNROLL)

    @pl.when(i == 0)
    def _():
        fetch(i)

    @pl.when(i + 1 < pl.num_programs(0))
    def _():
        fetch(i + 1)

    def wait(r, c):
        for k in range(MOE_TOPK):
            row_copy(i % 2, k, r, 0).wait()
        return c

    lax.fori_loop(0, tc, wait, 0, unroll=ROW_DMA_UNROLL)
    info = info_ref[...]
    y_lo = jnp.zeros((tc, half), F32)
    y_hi = jnp.zeros((tc, half), F32)
    for k in range(MOE_TOPK):
        lo, hi = _unpack_pair(buf_ref[i % 2, k])
        gate = info[:, MOE_TOPK + k:MOE_TOPK + k + 1]
        y_lo = y_lo + gate * lo
        y_hi = y_hi + gate * hi
    xn_lo = x_ref[:, :half] + gf_ref[:, :half] * y_lo
    xn_hi = x_ref[:, half:] + gf_ref[:, half:] * y_hi
    if final:
        ssq = jnp.sum(xn_lo * xn_lo, axis=-1, keepdims=True) + jnp.sum(xn_hi * xn_hi, axis=-1, keepdims=True)
        inv = lax.rsqrt(ssq / (2 * half) + NORM_EPS)
        xn_lo = xn_lo * inv * nf_ref[:, :half]
        xn_hi = xn_hi * inv * nf_ref[:, half:]
    o_ref[:, :half] = xn_lo
    o_ref[:, half:] = xn_hi


def _combine(y, pos_flat, x, info, mod, final_gain=None, tc=256):
    t, d = x.shape
    final = final_gain is not None
    in_specs = [
        pl.BlockSpec(memory_space=pl.ANY),
        pl.BlockSpec((tc, d), lambda i, pos: (i, 0)),
        pl.BlockSpec((tc, LANES), lambda i, pos: (i, 0)),
        pl.BlockSpec((None, 1, d), lambda i, pos: (mod.row(5, i * tc), 0, 0)),
    ]
    args = [pos_flat, y, x, info, mod.table]
    if final:
        in_specs.append(pl.BlockSpec((1, d), lambda i, pos: (0, 0)))
        args.append(final_gain.reshape(1, d))
    return pl.pallas_call(
        functools.partial(_combine_kernel, final=final),
        grid_spec=pltpu.PrefetchScalarGridSpec(
            num_scalar_prefetch=1,
            grid=(t // tc,),
            in_specs=in_specs,
            out_specs=pl.BlockSpec((tc, d), lambda i, pos: (i, 0)),
            scratch_shapes=[pltpu.VMEM((2, MOE_TOPK, tc, d // 2), U32),
                            pltpu.SemaphoreType.DMA((2,))],
        ),
        out_shape=jax.ShapeDtypeStruct((t, d), F32),
        compiler_params=_params(2 * tc * d * 4 + 4 * tc * d * 4 + 4 * tc * d * 4, 1),
        name="moe_combine",
    )(*args)


def _moe_ffn(x, gain, mod, router, wg, wu, wd, widx, final_gain=None, tm=512):
    t, d = x.shape
    n_exp = router.shape[1]
    wg, wu, wd = (w.reshape((-1,) + w.shape[2:]) for w in (wg, wu, wd))
    hf, info, cnt = _route(x, gain, mod, router)

    idx = info[:, 0:2].astype(I32)
    rank = info[:, 4:6].astype(I32)
    counts = cnt[0, :n_exp].astype(I32)
    padded = ((counts + tm - 1) // tm) * tm
    ends = jnp.cumsum(padded)
    starts = ends - padded
    pos = starts[idx] + rank
    p_max = MOE_TOPK * t + n_exp * tm
    tok = jnp.repeat(jnp.arange(t, dtype=I32), MOE_TOPK)
    sorted_tok = jnp.zeros((p_max,), I32).at[pos.reshape(-1)].set(
        tok, unique_indices=True, mode="promise_in_bounds")
    total = ends[-1:].astype(I32)
    n_tiles = total // tm
    tile_row = jnp.minimum(jnp.arange(p_max // tm, dtype=I32), n_tiles[0] - 1) * tm
    local_expert = jnp.minimum(
        jnp.sum((ends[None, :] <= tile_row[:, None]).astype(I32), axis=1), n_exp - 1)
    tile_expert = widx * n_exp + local_expert
    next_group = (ends // tm)[local_expert]

    xs = _gather_rows(hf, sorted_tok, total)
    act = _grouped_gateup(xs, wg, wu, tile_expert, n_tiles, next_group, tm)
    y = _grouped_down(act, wd, tile_expert, n_tiles, next_group, tm)
    return _combine(y, pos.reshape(-1), x, info, mod, final_gain)


def kernel(x, c, mod_w, mod_b, norm_mix, norm_ffn, norm_final, moba_wqkv, moba_wo, sc_w_in, sc_conv,
           sc_w_out, pool_w, pool_scale, cf_w1, cf_b1, cf_dw, cf_dw_b, cf_ln_g, cf_ln_b, cf_w2, cf_b2,
           ffn_wg, ffn_wu, ffn_wd, moe_router, moe_wg, moe_wu, moe_wd):
    batch, seq, d = x.shape
    depth = mod_w.shape[0]
    table = _modulation(c, mod_w, mod_b)
    xt = x.reshape(batch * seq, d)
    for layer in range(depth):
        mod = _Mod(table, layer, batch, seq)
        m, j = layer % 4, layer // 4
        if m == 0:
            h = _prenorm(xt, norm_mix[layer], mod, 0)
            qkv = _qkv_rope(h, moba_wqkv[j], seq)
            o = _moba_attention(qkv, batch, seq, d)
            xt = _proj_residual(o, moba_wo, j, xt, mod, 2)
        elif m == 1:
            h = _prenorm(xt, norm_mix[layer], mod, 0)
            v = _short_conv_in(h, sc_w_in[j], sc_conv[j], seq)
            xt = _proj_residual(v, sc_w_out, j, xt, mod, 2)
        elif m == 2:
            xt = _pool_mixer(xt, norm_mix[layer], mod, pool_w[j], pool_scale[j], seq)
        else:
            h = _prenorm(xt, norm_mix[layer], mod, 0)
            u = _conformer_glu(h, cf_w1[j], cf_b1[j])
            z = _conformer_conv(u, cf_dw[j], cf_dw_b[j], cf_ln_g[j], cf_ln_b[j], seq)
            xt = _proj_residual(z, cf_w2, j, xt, mod, 2, bias=cf_b2[j])

        i = layer // 2
        if layer % 2 == 0:
            h = _prenorm(xt, norm_ffn[layer], mod, 3)
            xt = _dense_ffn(xt, h, ffn_wg, ffn_wu, ffn_wd, i, mod)
        else:
            last = layer == depth - 1
            xt = _moe_ffn(xt, norm_ffn[layer], mod, moe_router[i], moe_wg, moe_wu, moe_wd, i,
                          final_gain=norm_final if last else None)
    if depth % 2 == 1:
        xt = _final_norm(xt, norm_final)
    return xt.reshape(batch, seq, d)
```

```python
import functools

import jax
import jax.numpy as jnp
from jax import lax
from jax.experimental import pallas as pl
from jax.experimental.pallas import tpu as pltpu

F32 = jnp.float32
BF16 = jnp.bfloat16
I32 = jnp.int32
U32 = jnp.uint32

HEAD_DIM = 128
MOBA_BLOCK = 256
MOBA_TOPK = 3
ROPE_THETA = 10000.0
POOL_WINDOWS = (2, 4, 8, 16)
MOE_TOPK = 2
NORM_EPS = 1e-6
LN_EPS = 1e-5
NEG_INF = -1e30

LANES = 128
SUBLANES = 8
VMEM_LIMIT_CAP = 60000 * 1024
MOD_PARTS = 6
STAGE_SLAB_BYTES = 12 << 20
CAST_ROWS = 256
ROW_CHUNK = 256


def _params(vmem_bytes, n_axes):
    limit = min(int(vmem_bytes) + (6 << 20), VMEM_LIMIT_CAP)
    return pltpu.CompilerParams(
        dimension_semantics=("arbitrary",) * n_axes, vmem_limit_bytes=limit)


def _dot(a, b):
    return jnp.dot(a, b, preferred_element_type=F32)


def _dot_nt(a, b):
    return lax.dot_general(a, b, (((1,), (1,)), ((), ())), preferred_element_type=F32)


def _silu(x):
    return x * jax.nn.sigmoid(x)


def _adaln(x, g, sc, sh):
    ms = jnp.mean(x * x, axis=-1, keepdims=True)
    return (x * lax.rsqrt(ms + NORM_EPS)) * g * (1.0 + sc) + sh


def _mod_kernel(c_ref, w_ref, b_ref, o_ref):
    c = c_ref[...]
    ca = _silu(c).astype(BF16)
    o_ref[...] = _dot(ca, w_ref[...].astype(BF16)) + b_ref[...]


def _modulation(c, mod_w, mod_b):
    depth, d, n = mod_w.shape
    b = c.shape[0]
    tn = 1024
    out = pl.pallas_call(
        _mod_kernel,
        grid=(depth, n // tn),
        in_specs=[
            pl.BlockSpec((b, d), lambda l, j: (0, 0)),
            pl.BlockSpec((None, d, tn), lambda l, j: (l, 0, j)),
            pl.BlockSpec((None, 1, tn), lambda l, j: (l, 0, j)),
        ],
        out_specs=pl.BlockSpec((None, b, tn), lambda l, j: (l, 0, j)),
        out_shape=jax.ShapeDtypeStruct((depth, b, n), F32),
        compiler_params=_params(2 * d * tn * 4 + d * tn * 2, 2),
        name="modulation",
    )(c, mod_w, mod_b.reshape(depth, 1, n))
    return out.reshape(depth * b * MOD_PARTS, 1, d)


class _Mod:
    def __init__(self, table, layer, batch, seq):
        self.table = table
        self.base = layer * batch * MOD_PARTS
        self.seq = seq

    def row(self, part, tok0):
        return self.base + (tok0 // self.seq) * MOD_PARTS + part


def _prenorm_kernel(x_ref, g_ref, sc_ref, sh_ref, o_ref):
    o_ref[...] = _adaln(x_ref[...], g_ref[...], sc_ref[...], sh_ref[...]).astype(o_ref.dtype)


def _prenorm(x, gain, mod, part0, tm=512):
    t, d = x.shape
    row = lambda p: pl.BlockSpec((None, 1, d), lambda i: (mod.row(p, i * tm), 0, 0))
    return pl.pallas_call(
        _prenorm_kernel,
        grid=(t // tm,),
        in_specs=[
            pl.BlockSpec((tm, d), lambda i: (i, 0)),
            pl.BlockSpec((1, d), lambda i: (0, 0)),
            row(part0 + 1), row(part0),
        ],
        out_specs=pl.BlockSpec((tm, d), lambda i: (i, 0)),
        out_shape=jax.ShapeDtypeStruct((t, d), BF16),
        compiler_params=_params(2 * tm * d * 6 + 3 * tm * d * 4, 1),
        name="prenorm",
    )(x, gain.reshape(1, d), mod.table, mod.table)


def _final_norm_kernel(x_ref, g_ref, o_ref):
    x = x_ref[...]
    ms = jnp.mean(x * x, axis=-1, keepdims=True)
    o_ref[...] = x * lax.rsqrt(ms + NORM_EPS) * g_ref[...]


def _final_norm(x, gain, tm=512):
    t, d = x.shape
    return pl.pallas_call(
        _final_norm_kernel,
        grid=(t // tm,),
        in_specs=[pl.BlockSpec((tm, d), lambda i: (i, 0)), pl.BlockSpec((1, d), lambda i: (0, 0))],
        out_specs=pl.BlockSpec((tm, d), lambda i: (i, 0)),
        out_shape=jax.ShapeDtypeStruct((t, d), F32),
        compiler_params=_params(6 * tm * d * 4, 1),
        name="final_norm",
    )(x, gain.reshape(1, d))


def _qkv_kernel(x_ref, gn_ref, sc_ref, sh_ref, w_ref, cos_ref, sin_ref, o_ref, wb_ref, *, nq, scale):
    j = pl.program_id(0)
    tm, tn = o_ref.shape

    @pl.when(pl.program_id(1) == 0)
    def _():
        wb_ref[...] = w_ref[...].astype(BF16)

    def tile(mult):
        for r0 in range(0, tm, ROW_CHUNK):
            rs = slice(r0, r0 + ROW_CHUNK)
            h = _adaln(x_ref[rs, :], gn_ref[...], sc_ref[...], sh_ref[...]).astype(BF16)
            acc = _dot(h, wb_ref[...])
            if mult is None:
                o_ref[rs, :] = acc.astype(o_ref.dtype)
                continue
            cos = cos_ref[rs, :]
            sin = sin_ref[rs, :]
            for c in range(tn // HEAD_DIM):
                a = acc[:, c * HEAD_DIM:(c + 1) * HEAD_DIM]
                r = a * cos + pltpu.roll(a, HEAD_DIM // 2, 1) * sin
                if mult != 1.0:
                    r = r * mult
                o_ref[rs, c * HEAD_DIM:(c + 1) * HEAD_DIM] = r.astype(o_ref.dtype)

    @pl.when(j < nq)
    def _():
        tile(scale)

    @pl.when((j >= nq) & (j < 2 * nq))
    def _():
        tile(1.0)

    @pl.when(j >= 2 * nq)
    def _():
        tile(None)


def _qkv_rope(x, gain, mod, w, seq, tm=1024, tn=1024):
    t, d = x.shape
    n = w.shape[1]
    tn = min(tn, d)
    half = HEAD_DIM // 2
    inv = ROPE_THETA ** (-jnp.arange(half, dtype=F32) * (2.0 / HEAD_DIM))
    ang = jnp.arange(seq, dtype=F32)[:, None] * inv[None, :]
    cos = jnp.concatenate([jnp.cos(ang), jnp.cos(ang)], axis=-1)
    sin = jnp.concatenate([-jnp.sin(ang), jnp.sin(ang)], axis=-1)
    tm = min(tm, seq)
    sblk = seq // tm
    kern = functools.partial(_qkv_kernel, nq=d // tn, scale=HEAD_DIM ** -0.5)
    row = lambda p: pl.BlockSpec((None, 1, d), lambda j, i: (mod.row(p, i * tm), 0, 0))
    return pl.pallas_call(
        kern,
        grid=(n // tn, t // tm),
        in_specs=[
            pl.BlockSpec((tm, d), lambda j, i: (i, 0)),
            pl.BlockSpec((1, d), lambda j, i: (0, 0)),
            row(1), row(0),
            pl.BlockSpec((d, tn), lambda j, i: (0, j)),
            pl.BlockSpec((tm, HEAD_DIM), lambda j, i: (i % sblk, 0)),
            pl.BlockSpec((tm, HEAD_DIM), lambda j, i: (i % sblk, 0)),
        ],
        out_specs=pl.BlockSpec((tm, tn), lambda j, i: (i, j)),
        out_shape=jax.ShapeDtypeStruct((t, n), BF16),
        scratch_shapes=[pltpu.VMEM((d, tn), BF16)],
        compiler_params=_params(
            2 * tm * d * 4 + 2 * d * tn * 4 + d * tn * 2 + 2 * tm * tn * 2
            + 4 * ROW_CHUNK * (d + tn) * 4, 2),
        name="qkv_rope",
    )(x, gain.reshape(1, d), mod.table, mod.table, w, cos, sin)


def _moba_kernel(q_ref, k_ref, v_ref, o_ref, kbar_ref, vt_ref, s_ref, p_ref, *, nb, blk, topk):
    kr = kbar_ref.shape[0]

    kbar_ref[...] = jnp.zeros(kbar_ref.shape, F32)
    for m in range(nb):
        rows = slice(m * blk, (m + 1) * blk)
        kbar_ref[m:m + 1, :] = jnp.mean(k_ref[rows, :].astype(F32), axis=0, keepdims=True)
        vt_ref[:, rows] = v_ref[rows, :].astype(F32).T.astype(BF16)
    kbar = kbar_ref[...]
    kb_hi = kbar.astype(BF16)
    kb_lo = (kbar - kb_hi.astype(F32)).astype(BF16)

    blk_id = lax.broadcasted_iota(I32, (kr, blk), 0)
    causal = lax.broadcasted_iota(I32, (blk, blk), 0) <= lax.broadcasted_iota(I32, (blk, blk), 1)

    for n in range(nb):
        q = q_ref[n * blk:(n + 1) * blk, :]
        if n > 0:
            gate = _dot_nt(kb_hi, q) + _dot_nt(kb_lo, q)
            valid = blk_id < n
            g = jnp.where(valid, gate, NEG_INF)
            cnt = jnp.zeros((kr, blk), F32)
            for mp in range(n):
                other = g[mp:mp + 1, :]
                cnt = cnt + jnp.where(other > g, 1.0, jnp.where((other == g) & (blk_id > mp), 1.0, 0.0))
            sel = jnp.where(valid & (cnt < topk), 1.0, 0.0)

        mx = None
        for m in range(n + 1):
            rows = slice(m * blk, (m + 1) * blk)
            s = _dot_nt(k_ref[rows, :], q)
            s = jnp.where(causal if m == n else sel[m:m + 1, :] > 0.0, s, NEG_INF)
            s_ref[rows, :] = s
            bm = jnp.max(s, axis=0, keepdims=True)
            mx = bm if mx is None else jnp.maximum(mx, bm)
        denom = jnp.zeros((1, blk), F32)
        for m in range(n + 1):
            rows = slice(m * blk, (m + 1) * blk)
            p = jnp.exp(s_ref[rows, :] - mx)
            denom = denom + jnp.sum(p, axis=0, keepdims=True)
            p_ref[rows, :] = p.astype(BF16)
        keys = (n + 1) * blk
        acc = _dot(vt_ref[:, 0:keys], p_ref[0:keys, :])
        o_ref[n * blk:(n + 1) * blk, :] = (acc / denom).T.astype(o_ref.dtype)


def _moba_attention(qkv, batch, seq, d):
    t = qkv.shape[0]
    heads = d // HEAD_DIM
    blk = MOBA_BLOCK
    assert seq % blk == 0
    nb = seq // blk
    kr = 16
    assert nb <= kr
    kern = functools.partial(_moba_kernel, nb=nb, blk=blk, topk=min(MOBA_TOPK, nb - 1))
    return pl.pallas_call(
        kern,
        grid=(batch, heads),
        in_specs=[
            pl.BlockSpec((seq, HEAD_DIM), lambda b, h: (b, h)),
            pl.BlockSpec((seq, HEAD_DIM), lambda b, h: (b, heads + h)),
            pl.BlockSpec((seq, HEAD_DIM), lambda b, h: (b, 2 * heads + h)),
        ],
        out_specs=pl.BlockSpec((seq, HEAD_DIM), lambda b, h: (b, h)),
        out_shape=jax.ShapeDtypeStruct((t, d), BF16),
        scratch_shapes=[pltpu.VMEM((kr, HEAD_DIM), F32), pltpu.VMEM((HEAD_DIM, seq), BF16),
                        pltpu.VMEM((seq, blk), F32), pltpu.VMEM((seq, blk), BF16)],
        compiler_params=_params(10 * seq * HEAD_DIM * 2 + seq * blk * 6 + 16 * blk * blk * 4, 2),
        name="moba_attention",
    )(qkv, qkv, qkv)


def _proj_res_kernel(*refs, has_bias):
    if has_bias:
        a_ref, w_ref, r_ref, g_ref, b_ref, o_ref, wb_ref = refs
    else:
        a_ref, w_ref, r_ref, g_ref, o_ref, wb_ref = refs

    @pl.when(pl.program_id(1) == 0)
    def _():
        wb_ref[...] = w_ref[...].astype(BF16)

    y = _dot(a_ref[...], wb_ref[...])
    if has_bias:
        y = y + b_ref[...]
    o_ref[...] = r_ref[...] + g_ref[...] * y


def _proj_residual(a, w3, widx, resid, mod, gate_part, bias=None, tm=1024, tn=512):
    t, k = a.shape
    n = w3.shape[2]
    in_specs = [
        pl.BlockSpec((tm, k), lambda j, i: (i, 0)),
        pl.BlockSpec((None, k, tn), lambda j, i: (widx, 0, j)),
        pl.BlockSpec((tm, tn), lambda j, i: (i, j)),
        pl.BlockSpec((None, 1, tn), lambda j, i: (mod.row(gate_part, i * tm), 0, j)),
    ]
    args = [a, w3, resid, mod.table]
    if bias is not None:
        in_specs.append(pl.BlockSpec((1, tn), lambda j, i: (0, j)))
        args.append(bias.reshape(1, n))
    return pl.pallas_call(
        functools.partial(_proj_res_kernel, has_bias=bias is not None),
        grid=(n // tn, t // tm),
        in_specs=in_specs,
        out_specs=pl.BlockSpec((tm, tn), lambda j, i: (i, j)),
        out_shape=jax.ShapeDtypeStruct((t, n), F32),
        scratch_shapes=[pltpu.VMEM((k, tn), BF16)],
        compiler_params=_params(
            2 * tm * k * 2 + 2 * k * tn * 4 + k * tn * 2 + 5 * tm * tn * 4, 2),
        name="proj_residual",
    )(*args)


def _shift_rows(p, halo, k):
    hr = halo.shape[0]
    body = pltpu.roll(p, k, 0)
    head = jnp.where(lax.broadcasted_iota(I32, (hr, p.shape[1]), 0) < k,
                     pltpu.roll(halo, k, 0), body[:hr])
    return head, body


def _shortconv_kernel(x_ref, wb_g, wc_g, wx_g, cw_ref, o_ref, wb_ref, halo_ref, *, width, tiles_per_seq):
    i = pl.program_id(1)

    @pl.when(i == 0)
    def _():
        wb_ref[0] = wb_g[...].astype(BF16)
        wb_ref[1] = wc_g[...].astype(BF16)
        wb_ref[2] = wx_g[...].astype(BF16)

    @pl.when(i % tiles_per_seq == 0)
    def _():
        halo_ref[...] = jnp.zeros(halo_ref.shape, F32)

    x = x_ref[...]
    gb = _dot(x, wb_ref[0])
    p = _dot(x, wb_ref[1]) * _dot(x, wb_ref[2])
    hr = halo_ref.shape[0]
    halo = halo_ref[...]
    cw = cw_ref[...]
    body = cw[width - 1:width, :] * p
    head = body[:hr]
    for s in range(1, width):
        hs, bs = _shift_rows(p, halo, s)
        w = cw[width - 1 - s:width - s, :]
        body = body + w * bs
        head = head + w * hs
    o_ref[...] = (gb * body).astype(o_ref.dtype)
    o_ref[0:hr, :] = (gb[:hr] * head).astype(o_ref.dtype)
    halo_ref[...] = p[p.shape[0] - hr:, :]


def _short_conv_in(h, w_in, conv_w, seq, tm=1024, tn=256):
    t, d = h.shape
    width = conv_w.shape[0]
    nd = d // tn
    hr = 16
    assert width - 1 <= hr
    kern = functools.partial(_shortconv_kernel, width=width, tiles_per_seq=seq // tm)
    wspec = lambda off: pl.BlockSpec((d, tn), lambda j, i: (0, off * nd + j))
    return pl.pallas_call(
        kern,
        grid=(nd, t // tm),
        in_specs=[
            pl.BlockSpec((tm, d), lambda j, i: (i, 0)),
            wspec(0), wspec(1), wspec(2),
            pl.BlockSpec((width, tn), lambda j, i: (0, j)),
        ],
        out_specs=pl.BlockSpec((tm, tn), lambda j, i: (i, j)),
        out_shape=jax.ShapeDtypeStruct((t, d), BF16),
        scratch_shapes=[pltpu.VMEM((3, d, tn), BF16), pltpu.VMEM((hr, tn), F32)],
        compiler_params=_params(
            2 * tm * d * 2 + 6 * d * tn * 4 + 3 * d * tn * 2 + 10 * tm * tn * 4, 2),
        name="short_conv_in",
    )(h, w_in, w_in, w_in, conv_w)


def _pool_kernel(x_ref, xh_ref, gn_ref, sc_ref, sh_ref, w_ref, ps_ref, ga_ref, o_ref,
                 wb_ref, ext_ref, *, windows, tiles_per_seq):
    i = pl.program_id(0)
    tm, d = x_ref.shape
    hr = xh_ref.shape[0]
    cg = d // len(windows)

    @pl.when(i == 0)
    def _():
        wb_ref[...] = w_ref[...].astype(BF16)

    x = x_ref[...]
    gn, sc, sh = gn_ref[...], sc_ref[...], sh_ref[...]
    h = _adaln(x, gn, sc, sh)
    hh = _adaln(xh_ref[...], gn, sc, sh)
    first = i % tiles_per_seq == 0
    ext_ref[0:hr, :] = jnp.where(first, 0.0, hh)
    ext_ref[hr:, :] = h
    pos = (i % tiles_per_seq) * tm + lax.broadcasted_iota(I32, (tm, 1), 0)
    for g, w in enumerate(windows):
        cols = slice(g * cg, (g + 1) * cg)
        s = ext_ref[:, cols]
        step = 1
        while step < w:
            s = s + pltpu.roll(s, step, 0)
            step *= 2
        count = jnp.minimum(pos + 1, w).astype(F32)
        pooled = s[hr:] / count - h[:, cols]
        mixed = _dot(pooled.astype(BF16), wb_ref[g])
        o_ref[:, cols] = x[:, cols] + ga_ref[:, cols] * (mixed * ps_ref[:, cols])


def _pool_mixer(x, gain, mod, w_group, scale, seq, tm=256):
    t, d = x.shape
    g, cg, _ = w_group.shape
    assert g == len(POOL_WINDOWS) and cg % LANES == 0
    hr = 16
    assert max(POOL_WINDOWS) <= hr and all(w & (w - 1) == 0 for w in POOL_WINDOWS)
    kern = functools.partial(_pool_kernel, windows=POOL_WINDOWS, tiles_per_seq=seq // tm)
    row = lambda p: pl.BlockSpec((None, 1, d), lambda i: (mod.row(p, i * tm), 0, 0))
    return pl.pallas_call(
        kern,
        grid=(t // tm,),
        in_specs=[
            pl.BlockSpec((tm, d), lambda i: (i, 0)),
            pl.BlockSpec((hr, d), lambda i: (jnp.maximum(i * (tm // hr) - 1, 0), 0)),
            pl.BlockSpec((1, d), lambda i: (0, 0)),
            row(1), row(0),
            pl.BlockSpec((g, cg, cg), lambda i: (0, 0, 0)),
            pl.BlockSpec((1, d), lambda i: (0, 0)),
            row(2),
        ],
        out_specs=pl.BlockSpec((tm, d), lambda i: (i, 0)),
        out_shape=jax.ShapeDtypeStruct((t, d), F32),
        scratch_shapes=[pltpu.VMEM((g, cg, cg), BF16), pltpu.VMEM((tm + hr, d), F32)],
        compiler_params=_params(4 * tm * d * 4 + 2 * g * cg * cg * 4 + g * cg * cg * 2
                                + 6 * tm * d * 4, 1),
        name="pool_mixer",
    )(x, x, gain.reshape(1, d), mod.table, mod.table, w_group, scale.reshape(1, d), mod.table)


def _glu_kernel(x_ref, gn_ref, sc_ref, sh_ref, wa_ref, wg_ref, ba_ref, bg_ref, o_ref, wb_ref):
    @pl.when(pl.program_id(1) == 0)
    def _():
        wb_ref[0] = wa_ref[...].astype(BF16)
        wb_ref[1] = wg_ref[...].astype(BF16)

    for r0 in range(0, x_ref.shape[0], ROW_CHUNK):
        rs = slice(r0, r0 + ROW_CHUNK)
        h = _adaln(x_ref[rs, :], gn_ref[...], sc_ref[...], sh_ref[...]).astype(BF16)
        a = _dot(h, wb_ref[0]) + ba_ref[...]
        g = _dot(h, wb_ref[1]) + bg_ref[...]
        o_ref[rs, :] = a * jax.nn.sigmoid(g)


def _conformer_glu(x, gain, mod, w1, b1, tm=1024, tn=512):
    t, d = x.shape
    n = w1.shape[1] // 2
    nd = n // tn
    b1 = b1.reshape(1, 2 * n)
    row = lambda p: pl.BlockSpec((None, 1, d), lambda j, i: (mod.row(p, i * tm), 0, 0))
    return pl.pallas_call(
        _glu_kernel,
        grid=(nd, t // tm),
        in_specs=[
            pl.BlockSpec((tm, d), lambda j, i: (i, 0)),
            pl.BlockSpec((1, d), lambda j, i: (0, 0)),
            row(1), row(0),
            pl.BlockSpec((d, tn), lambda j, i: (0, j)),
            pl.BlockSpec((d, tn), lambda j, i: (0, nd + j)),
            pl.BlockSpec((1, tn), lambda j, i: (0, j)),
            pl.BlockSpec((1, tn), lambda j, i: (0, nd + j)),
        ],
        out_specs=pl.BlockSpec((tm, tn), lambda j, i: (i, j)),
        out_shape=jax.ShapeDtypeStruct((t, n), F32),
        scratch_shapes=[pltpu.VMEM((2, d, tn), BF16)],
        compiler_params=_params(
            2 * tm * d * 4 + 4 * d * tn * 4 + 2 * d * tn * 2 + 2 * tm * tn * 4
            + 4 * ROW_CHUNK * (d + tn) * 4, 2),
        name="conformer_glu",
    )(x, gain.reshape(1, d), mod.table, mod.table, w1, w1, b1, b1)


def _cfconv_kernel(u_ref, uh_ref, dw_ref, dwb_ref, lg_ref, lb_ref, o_ref, ph_ref, y_ref,
                   *, width, tiles_per_seq, rc, cc):
    i = pl.program_id(0)
    tm, d = u_ref.shape
    hr = uh_ref.shape[0]
    rows = tm + hr
    first = i % tiles_per_seq == 0
    off = hr - (width - 1)
    for c0 in range(0, d, cc):
        cols = slice(c0, c0 + cc)
        ph_ref[0, 0:hr, :] = jnp.where(first, 0.0, uh_ref[:, cols])
        ph_ref[0, hr:, :] = u_ref[:, cols]
        e0 = ph_ref[0]
        for s in range(1, SUBLANES):
            ph_ref[s] = pltpu.roll(e0, rows - s, 0)
        w = dw_ref[:, cols]
        for r0 in range(0, tm, rc):
            acc = jnp.zeros((rc, cc), F32) + dwb_ref[:, cols]
            for k in range(width):
                q, s = divmod(off + k, SUBLANES)
                a0 = r0 + q * SUBLANES
                acc = acc + w[k:k + 1, :] * ph_ref[s, a0:a0 + rc, :]
            y_ref[r0:r0 + rc, cols] = acc
    y = y_ref[...]
    mu = jnp.mean(y, axis=-1, keepdims=True)
    yc = y - mu
    var = jnp.mean(yc * yc, axis=-1, keepdims=True)
    z = yc * lax.rsqrt(var + LN_EPS) * lg_ref[...] + lb_ref[...]
    o_ref[...] = _silu(z).astype(o_ref.dtype)


def _conformer_conv(u, dw, dw_b, ln_g, ln_b, seq, tm=256):
    t, d = u.shape
    width = dw.shape[0]
    hr = 32
    assert width - 1 <= hr
    cc = min(256, d)
    kern = functools.partial(_cfconv_kernel, width=width, tiles_per_seq=seq // tm,
                             rc=min(64, tm), cc=cc)
    vec = lambda: pl.BlockSpec((1, d), lambda i: (0, 0))
    return pl.pallas_call(
        kern,
        grid=(t // tm,),
        in_specs=[
            pl.BlockSpec((tm, d), lambda i: (i, 0)),
            pl.BlockSpec((hr, d), lambda i: (jnp.maximum(i * (tm // hr) - 1, 0), 0)),
            pl.BlockSpec((width, d), lambda i: (0, 0)),
            vec(), vec(), vec(),
        ],
        out_specs=pl.BlockSpec((tm, d), lambda i: (i, 0)),
        out_shape=jax.ShapeDtypeStruct((t, d), BF16),
        scratch_shapes=[pltpu.VMEM((SUBLANES, tm + hr, cc), F32), pltpu.VMEM((tm, d), F32)],
        compiler_params=_params(8 * tm * d * 4 + SUBLANES * (tm + hr) * cc * 4, 1),
        name="conformer_conv",
    )(u, u, dw, dw_b.reshape(1, d), ln_g.reshape(1, d), ln_b.reshape(1, d))


def _pick_cols(n, k):
    best = None
    for c in range(LANES, n + 1, LANES):
        if n % c == 0 and k * c * 4 <= STAGE_SLAB_BYTES:
            best = c
    assert best is not None
    return best


def _stage_weights(te_ref, nt_ref, nxt_ref, slabs, stage_ref, wb_ref, sem):
    j, i = pl.program_id(0), pl.program_id(1)
    k, tn = stage_ref.shape[1], stage_ref.shape[2]

    def copies(e, jj):
        return [pltpu.make_async_copy(
            w.at[e, :, pl.ds(pl.multiple_of(col(jj) * tn, LANES), tn)], stage_ref.at[s], sem.at[s])
            for s, (w, col) in enumerate(slabs)]

    new_w = (i == 0) | (te_ref[i] != te_ref[jnp.maximum(i - 1, 0)])

    @pl.when(new_w)
    def _():
        @pl.when((i == 0) & (j == 0))
        def _():
            for c in copies(te_ref[0], 0):
                c.start()

        for c in copies(te_ref[i], j):
            c.wait()

        def cast(c, carry):
            r = pl.multiple_of(c * CAST_ROWS, CAST_ROWS)
            for s in range(len(slabs)):
                wb_ref[s, pl.ds(r, CAST_ROWS), :] = stage_ref[s, pl.ds(r, CAST_ROWS), :].astype(BF16)
            return carry

        lax.fori_loop(0, k // CAST_ROWS, cast, 0)

        nxt = nxt_ref[i]
        more = nxt < nt_ref[0]

        @pl.when(more)
        def _():
            for c in copies(te_ref[nxt], j):
                c.start()

        @pl.when(jnp.logical_not(more) & (j + 1 < pl.num_programs(0)))
        def _():
            for c in copies(te_ref[0], j + 1):
                c.start()


def _gateup_kernel(te_ref, nt_ref, nxt_ref, rv_ref, x_ref, gn_ref, sc_ref, sh_ref, wg_hbm, wu_hbm,
                   o_ref, stage_ref, wb_ref, sem, *, norm):
    i = pl.program_id(1)
    same = lambda jj: jj
    _stage_weights(te_ref, nt_ref, nxt_ref, [(wg_hbm, same), (wu_hbm, same)], stage_ref, wb_ref, sem)
    valid = rv_ref[i]
    for r0 in range(0, x_ref.shape[0], ROW_CHUNK):
        rs = slice(r0, r0 + ROW_CHUNK)

        @pl.when(r0 < valid)
        def _():
            x = x_ref[rs, :]
            if norm:
                x = _adaln(x, gn_ref[...], sc_ref[...], sh_ref[...]).astype(BF16)
            g = _dot(x, wb_ref[0])
            u = _dot(x, wb_ref[1])
            o_ref[rs, :] = (_silu(g) * u).astype(o_ref.dtype)

        @pl.when(r0 >= valid)
        def _():
            o_ref[rs, :] = jnp.zeros((ROW_CHUNK, o_ref.shape[1]), o_ref.dtype)


def _grouped_gateup(xs, wg, wu, tile_expert, n_tiles, next_group, rows_valid, tm, gain, mod, norm):
    p, d = xs.shape
    f = wg.shape[2]
    assert d % CAST_ROWS == 0 and tm % ROW_CHUNK == 0
    tf = _pick_cols(f, d)
    row = lambda part: pl.BlockSpec(
        (None, 1, d), lambda j, i, te, nt, nx, rv: (mod.row(part, i * tm if norm else 0), 0, 0))
    return pl.pallas_call(
        functools.partial(_gateup_kernel, norm=norm),
        grid_spec=pltpu.PrefetchScalarGridSpec(
            num_scalar_prefetch=4,
            grid=(f // tf, p // tm),
            in_specs=[
                pl.BlockSpec((tm, d), lambda j, i, te, nt, nx, rv: (jnp.minimum(i, nt[0] - 1), 0)),
                pl.BlockSpec((1, d), lambda j, i, te, nt, nx, rv: (0, 0)),
                row(4), row(3),
                pl.BlockSpec(memory_space=pltpu.HBM),
                pl.BlockSpec(memory_space=pltpu.HBM),
            ],
            out_specs=pl.BlockSpec((tm, tf), lambda j, i, te, nt, nx, rv: (i, j)),
            scratch_shapes=[pltpu.VMEM((2, d, tf), F32), pltpu.VMEM((2, d, tf), BF16),
                            pltpu.SemaphoreType.DMA((2,))],
        ),
        out_shape=jax.ShapeDtypeStruct((p, f), BF16),
        compiler_params=_params(
            2 * d * tf * 6 + 2 * tm * d * xs.dtype.itemsize + 2 * tm * tf * 2
            + 4 * ROW_CHUNK * (tf + d) * 4, 2),
        name="grouped_gateup",
    )(tile_expert, n_tiles, next_group, rows_valid, xs, gain.reshape(1, d), mod.table, mod.table, wg, wu)


def _pack_pair(lo, hi):
    bl = lax.bitcast_convert_type(lo.astype(BF16).astype(F32), U32) >> 16
    bh = lax.bitcast_convert_type(hi.astype(BF16).astype(F32), U32) & jnp.uint32(0xFFFF0000)
    return bh | bl


def _unpack_pair(w):
    lo = lax.bitcast_convert_type(w << 16, F32)
    hi = lax.bitcast_convert_type(w & jnp.uint32(0xFFFF0000), F32)
    return lo, hi


def _down_kernel(te_ref, nt_ref, nxt_ref, rv_ref, a_ref, w_hbm, o_ref, stage_ref, wb_ref, sem, *, half_blocks):
    i = pl.program_id(1)
    slabs = [(w_hbm, lambda jj: jj), (w_hbm, lambda jj: half_blocks + jj)]
    _stage_weights(te_ref, nt_ref, nxt_ref, slabs, stage_ref, wb_ref, sem)
    valid = rv_ref[i]
    for r0 in range(0, a_ref.shape[0], ROW_CHUNK):
        rs = slice(r0, r0 + ROW_CHUNK)

        @pl.when(r0 < valid)
        def _():
            a = a_ref[rs, :]
            o_ref[rs, :] = _pack_pair(_dot(a, wb_ref[0]), _dot(a, wb_ref[1]))

        @pl.when(r0 >= valid)
        def _():
            o_ref[rs, :] = jnp.zeros((ROW_CHUNK, o_ref.shape[1]), o_ref.dtype)


def _grouped_down(act, wd, tile_expert, n_tiles, next_group, rows_valid, tm):
    p, f = act.shape
    n = wd.shape[2]
    assert f % CAST_ROWS == 0 and tm % ROW_CHUNK == 0
    tn = _pick_cols(n // 2, f)
    half_blocks = (n // 2) // tn
    return pl.pallas_call(
        functools.partial(_down_kernel, half_blocks=half_blocks),
        grid_spec=pltpu.PrefetchScalarGridSpec(
            num_scalar_prefetch=4,
            grid=(half_blocks, p // tm),
            in_specs=[
                pl.BlockSpec((tm, f), lambda j, i, te, nt, nx, rv: (jnp.minimum(i, nt[0] - 1), 0)),
                pl.BlockSpec(memory_space=pltpu.HBM),
            ],
            out_specs=pl.BlockSpec((tm, tn), lambda j, i, te, nt, nx, rv: (i, j)),
            scratch_shapes=[pltpu.VMEM((2, f, tn), F32), pltpu.VMEM((2, f, tn), BF16),
                            pltpu.SemaphoreType.DMA((2,))],
        ),
        out_shape=jax.ShapeDtypeStruct((p, n // 2), U32),
        compiler_params=_params(2 * f * tn * 6 + 2 * tm * f * 2 + 2 * tm * tn * 4 + 8 * ROW_CHUNK * tn * 4, 2),
        name="grouped_down",
    )(tile_expert, n_tiles, next_group, rows_valid, act, wd)


def _dense_ffn(x, gain, wg, wu, wd, widx, mod, tm=512):
    t = x.shape[0]
    nt = t // tm
    te = jnp.full((nt,), widx, I32)
    single_group = jnp.full((nt,), nt, I32)
    all_rows = jnp.full((nt,), tm, I32)
    act = _grouped_gateup(x, wg, wu, te, jnp.full((1,), nt, I32), single_group, all_rows, tm,
                          gain, mod, norm=True)
    return _proj_residual(act, wd, widx, x, mod, 5, tm=tm, tn=512)


def _router_kernel(x_ref, gn_ref, sc_ref, sh_ref, r_ref, h_ref, info_ref, cnt_ref, run_ref, *, n_exp):
    i = pl.program_id(0)
    tm = x_ref.shape[0]

    @pl.when(i == 0)
    def _():
        run_ref[...] = jnp.zeros(run_ref.shape, F32)

    h = _adaln(x_ref[...], gn_ref[...], sc_ref[...], sh_ref[...])
    half = h.shape[1] // 2
    h_ref[...] = _pack_pair(h[:, :half], h[:, half:])
    h_hi = h.astype(BF16)
    h_lo = (h - h_hi.astype(F32)).astype(BF16)
    r = r_ref[...]
    r_hi = r.astype(BF16)
    r_lo = (r - r_hi.astype(F32)).astype(BF16)
    logits = _dot(h_hi, r_hi) + (_dot(h_lo, r_hi) + _dot(h_hi, r_lo))

    lane = lax.broadcasted_iota(I32, (tm, LANES), 1).astype(F32)
    lg = jnp.where(lane < n_exp, logits, -jnp.inf)
    v1 = jnp.max(lg, axis=1, keepdims=True)
    i1 = jnp.min(jnp.where(lg == v1, lane, float(LANES)), axis=1, keepdims=True)
    lg2 = jnp.where(lane == i1, -jnp.inf, lg)
    v2 = jnp.max(lg2, axis=1, keepdims=True)
    i2 = jnp.min(jnp.where(lg2 == v2, lane, float(LANES)), axis=1, keepdims=True)
    e = jnp.exp(v2 - v1)
    g1 = 1.0 / (1.0 + e)
    g2 = e / (1.0 + e)

    oh = jnp.where((lane == i1) | (lane == i2), 1.0, 0.0)
    tri = jnp.where(lax.broadcasted_iota(I32, (tm, tm), 0) > lax.broadcasted_iota(I32, (tm, tm), 1),
                    1.0, 0.0).astype(BF16)
    cum = _dot(tri, oh.astype(BF16)) + run_ref[...]
    rank1 = jnp.sum(jnp.where(lane == i1, cum, 0.0), axis=1, keepdims=True)
    rank2 = jnp.sum(jnp.where(lane == i2, cum, 0.0), axis=1, keepdims=True)
    run = run_ref[...] + jnp.sum(oh, axis=0, keepdims=True)
    run_ref[...] = run
    cnt_ref[...] = jnp.broadcast_to(run, cnt_ref.shape)

    info = jnp.zeros((tm, LANES), F32)
    for k, val in enumerate((i1, i2, g1, g2, rank1, rank2)):
        info = jnp.where(lane == float(k), val, info)
    info_ref[...] = info


def _route(x, gain, mod, router, tm=256):
    t, d = x.shape
    n_exp = router.shape[1]
    assert n_exp <= LANES
    rpad = jnp.pad(router, ((0, 0), (0, LANES - n_exp)))
    row = lambda p: pl.BlockSpec((None, 1, d), lambda i: (mod.row(p, i * tm), 0, 0))
    return pl.pallas_call(
        functools.partial(_router_kernel, n_exp=n_exp),
        grid=(t // tm,),
        in_specs=[
            pl.BlockSpec((tm, d), lambda i: (i, 0)),
            pl.BlockSpec((1, d), lambda i: (0, 0)),
            row(4), row(3),
            pl.BlockSpec((d, LANES), lambda i: (0, 0)),
        ],
        out_specs=[
            pl.BlockSpec((tm, d // 2), lambda i: (i, 0)),
            pl.BlockSpec((tm, LANES), lambda i: (i, 0)),
            pl.BlockSpec((8, LANES), lambda i: (0, 0)),
        ],
        out_shape=[
            jax.ShapeDtypeStruct((t, d // 2), U32),
            jax.ShapeDtypeStruct((t, LANES), F32),
            jax.ShapeDtypeStruct((8, LANES), F32),
        ],
        scratch_shapes=[pltpu.VMEM((1, LANES), F32)],
        compiler_params=_params(4 * tm * d * 4 + 4 * tm * d * 4 + 2 * d * LANES * 4, 1),
        name="moe_route",
    )(x, gain.reshape(1, d), mod.table, mod.table, rpad)


ROW_DMA_UNROLL = 8


def _gather_kernel(tok_ref, tot_ref, hp_ref, o_ref, buf_ref):
    i = pl.program_id(0)
    tg, half = buf_ref.shape
    base = i * tg

    @pl.when(base < tot_ref[0])
    def _():
        def copy_row(r, c):
            buf_ref[pl.ds(r, 1), :] = hp_ref[pl.ds(tok_ref[base + r], 1), :]
            return c

        lax.fori_loop(0, tg, copy_row, 0, unroll=ROW_DMA_UNROLL)
        lo, hi = _unpack_pair(buf_ref[...])
        o_ref[:, :half] = lo.astype(o_ref.dtype)
        o_ref[:, half:] = hi.astype(o_ref.dtype)

    @pl.when(base >= tot_ref[0])
    def _():
        o_ref[...] = jnp.zeros(o_ref.shape, o_ref.dtype)


def _gather_rows(hp, sorted_tok, total_rows, tg=256):
    t, half = hp.shape
    p = sorted_tok.shape[0]
    return pl.pallas_call(
        _gather_kernel,
        grid_spec=pltpu.PrefetchScalarGridSpec(
            num_scalar_prefetch=2,
            grid=(p // tg,),
            in_specs=[pl.BlockSpec(memory_space=pltpu.VMEM)],
            out_specs=pl.BlockSpec((tg, 2 * half), lambda i, tok, tot: (i, 0)),
            scratch_shapes=[pltpu.VMEM((tg, half), U32)],
        ),
        out_shape=jax.ShapeDtypeStruct((p, 2 * half), BF16),
        compiler_params=_params(t * half * 4 + 8 * tg * half * 4, 1),
        name="moe_gather",
    )(sorted_tok, total_rows, hp)


def _combine_kernel(*refs, final):
    if final:
        pos_ref, y_hbm, x_ref, info_ref, gf_ref, nf_ref, o_ref, buf_ref, sem = refs
    else:
        pos_ref, y_hbm, x_ref, info_ref, gf_ref, o_ref, buf_ref, sem = refs
    i = pl.program_id(0)
    tc, half = buf_ref.shape[2], buf_ref.shape[3]

    def row_copy(slot, k, r, src_row):
        return pltpu.make_async_copy(
            y_hbm.at[pl.ds(src_row, 1)], buf_ref.at[slot, k, pl.ds(r, 1)], sem.at[slot])

    def fetch(tile):
        base = tile * tc

        def start(r, c):
            for k in range(MOE_TOPK):
                row_copy(tile % 2, k, r, pos_ref[MOE_TOPK * (base + r) + k]).start()
            return c

        lax.fori_loop(0, tc, start, 0, unroll=ROW_DMA_UNROLL)

    @pl.when(i == 0)
    def _():
        fetch(i)

    @pl.when(i + 1 < pl.num_programs(0))
    def _():
        fetch(i + 1)

    def wait(r, c):
        for k in range(MOE_TOPK):
            row_copy(i % 2, k, r, 0).wait()
        return c

    lax.fori_loop(0, tc, wait, 0, unroll=ROW_DMA_UNROLL)
    info = info_ref[...]
    y_lo = jnp.zeros((tc, half), F32)
    y_hi = jnp.zeros((tc, half), F32)
    for k in range(MOE_TOPK):
        lo, hi = _unpack_pair(buf_ref[i % 2, k])
        gate = info[:, MOE_TOPK + k:MOE_TOPK + k + 1]
        y_lo = y_lo + gate * lo
        y_hi = y_hi + gate * hi
    xn_lo = x_ref[:, :half] + gf_ref[:, :half] * y_lo
    xn_hi = x_ref[:, half:] + gf_ref[:, half:] * y_hi
    if final:
        ssq = jnp.sum(xn_lo * xn_lo, axis=-1, keepdims=True) + jnp.sum(xn_hi * xn_hi, axis=-1, keepdims=True)
        inv = lax.rsqrt(ssq / (2 * half) + NORM_EPS)
        xn_lo = xn_lo * inv * nf_ref[:, :half]
        xn_hi = xn_hi * inv * nf_ref[:, half:]
    o_ref[:, :half] = xn_lo
    o_ref[:, half:] = xn_hi


def _combine(y, pos_flat, x, info, mod, final_gain=None, tc=256):
    t, d = x.shape
    final = final_gain is not None
    in_specs = [
        pl.BlockSpec(memory_space=pl.ANY),
        pl.BlockSpec((tc, d), lambda i, pos: (i, 0)),
        pl.BlockSpec((tc, LANES), lambda i, pos: (i, 0)),
        pl.BlockSpec((None, 1, d), lambda i, pos: (mod.row(5, i * tc), 0, 0)),
    ]
    args = [pos_flat, y, x, info, mod.table]
    if final:
        in_specs.append(pl.BlockSpec((1, d), lambda i, pos: (0, 0)))
        args.append(final_gain.reshape(1, d))
    return pl.pallas_call(
        functools.partial(_combine_kernel, final=final),
        grid_spec=pltpu.PrefetchScalarGridSpec(
            num_scalar_prefetch=1,
            grid=(t // tc,),
            in_specs=in_specs,
            out_specs=pl.BlockSpec((tc, d), lambda i, pos: (i, 0)),
            scratch_shapes=[pltpu.VMEM((2, MOE_TOPK, tc, d // 2), U32),
                            pltpu.SemaphoreType.DMA((2,))],
        ),
        out_shape=jax.ShapeDtypeStruct((t, d), F32),
        compiler_params=_params(2 * tc * d * 4 + 4 * tc * d * 4 + 4 * tc * d * 4, 1),
        name="moe_combine",
    )(*args)


def _moe_ffn(x, gain, mod, router, wg, wu, wd, widx, final_gain=None, tm=512):
    t, d = x.shape
    n_exp = router.shape[1]
    wg, wu, wd = (w.reshape((-1,) + w.shape[2:]) for w in (wg, wu, wd))
    hf, info, cnt = _route(x, gain, mod, router)

    idx = info[:, 0:2].astype(I32)
    rank = info[:, 4:6].astype(I32)
    counts = cnt[0, :n_exp].astype(I32)
    padded = ((counts + tm - 1) // tm) * tm
    ends = jnp.cumsum(padded)
    starts = ends - padded
    pos = starts[idx] + rank
    p_max = MOE_TOPK * t + n_exp * tm
    tok = jnp.repeat(jnp.arange(t, dtype=I32), MOE_TOPK)
    sorted_tok = jnp.zeros((p_max,), I32).at[pos.reshape(-1)].set(
        tok, unique_indices=True, mode="promise_in_bounds")
    total = ends[-1:].astype(I32)
    n_tiles = total // tm
    tile_row = jnp.minimum(jnp.arange(p_max // tm, dtype=I32), n_tiles[0] - 1) * tm
    local_expert = jnp.minimum(
        jnp.sum((ends[None, :] <= tile_row[:, None]).astype(I32), axis=1), n_exp - 1)
    tile_expert = widx * n_exp + local_expert
    next_group = (ends // tm)[local_expert]
    tile_ids = jnp.arange(p_max // tm, dtype=I32)
    rows_valid = jnp.where(
        tile_ids < n_tiles[0],
        jnp.clip((starts + counts)[local_expert] - tile_ids * tm, 0, tm), 0).astype(I32)

    xs = _gather_rows(hf, sorted_tok, total)
    act = _grouped_gateup(xs, wg, wu, tile_expert, n_tiles, next_group, rows_valid, tm,
                          gain, mod, norm=False)
    y = _grouped_down(act, wd, tile_expert, n_tiles, next_group, rows_valid, tm)
    return _combine(y, pos.reshape(-1), x, info, mod, final_gain)


def kernel(x, c, mod_w, mod_b, norm_mix, norm_ffn, norm_final, moba_wqkv, moba_wo, sc_w_in, sc_conv,
           sc_w_out, pool_w, pool_scale, cf_w1, cf_b1, cf_dw, cf_dw_b, cf_ln_g, cf_ln_b, cf_w2, cf_b2,
           ffn_wg, ffn_wu, ffn_wd, moe_router, moe_wg, moe_wu, moe_wd):
    batch, seq, d = x.shape
    depth = mod_w.shape[0]
    table = _modulation(c, mod_w, mod_b)
    xt = x.reshape(batch * seq, d)
    for layer in range(depth):
        mod = _Mod(table, layer, batch, seq)
        m, j = layer % 4, layer // 4
        if m == 0:
            qkv = _qkv_rope(xt, norm_mix[layer], mod, moba_wqkv[j], seq)
            o = _moba_attention(qkv, batch, seq, d)
            xt = _proj_residual(o, moba_wo, j, xt, mod, 2)
        elif m == 1:
            h = _prenorm(xt, norm_mix[layer], mod, 0)
            v = _short_conv_in(h, sc_w_in[j], sc_conv[j], seq)
            xt = _proj_residual(v, sc_w_out, j, xt, mod, 2)
        elif m == 2:
            xt = _pool_mixer(xt, norm_mix[layer], mod, pool_w[j], pool_scale[j], seq)
        else:
            u = _conformer_glu(xt, norm_mix[layer], mod, cf_w1[j], cf_b1[j])
            z = _conformer_conv(u, cf_dw[j], cf_dw_b[j], cf_ln_g[j], cf_ln_b[j], seq)
            xt = _proj_residual(z, cf_w2, j, xt, mod, 2, bias=cf_b2[j])

        i = layer // 2
        if layer % 2 == 0:
            xt = _dense_ffn(xt, norm_ffn[layer], ffn_wg, ffn_wu, ffn_wd, i, mod)
        else:
            last = layer == depth - 1
            xt = _moe_ffn(xt, norm_ffn[layer], mod, moe_router[i], moe_wg, moe_wu, moe_wd, i,
                          final_gain=norm_final if last else None)
    if depth % 2 == 1:
        xt = _final_norm(xt, norm_final)
    return xt.reshape(batch, seq, d)
```

```python
import functools

import jax
import jax.numpy as jnp
from jax import lax
from jax.experimental import pallas as pl
from jax.experimental.pallas import tpu as pltpu

F32 = jnp.float32
BF16 = jnp.bfloat16
I32 = jnp.int32
U32 = jnp.uint32

HEAD_DIM = 128
MOBA_BLOCK = 256
MOBA_TOPK = 3
ROPE_THETA = 10000.0
POOL_WINDOWS = (2, 4, 8, 16)
MOE_TOPK = 2
NORM_EPS = 1e-6
LN_EPS = 1e-5
NEG_INF = -1e30

LANES = 128
SUBLANES = 8
MXU_COLS = 256
VMEM_LIMIT_CAP = 60000 * 1024
MOD_PARTS = 6
STAGE_SLAB_BYTES = 12 << 20
CAST_ROWS = 256
ROW_CHUNK = 256


def _params(vmem_bytes, n_axes):
    limit = min(int(vmem_bytes) + (6 << 20), VMEM_LIMIT_CAP)
    return pltpu.CompilerParams(
        dimension_semantics=("arbitrary",) * n_axes, vmem_limit_bytes=limit)


def _dot(a, b):
    return jnp.dot(a, b, preferred_element_type=F32)


def _dot_nt(a, b):
    return lax.dot_general(a, b, (((1,), (1,)), ((), ())), preferred_element_type=F32)


def _silu(x):
    return x * jax.nn.sigmoid(x)


def _adaln(x, g, sc, sh):
    ms = jnp.mean(x * x, axis=-1, keepdims=True)
    return (x * lax.rsqrt(ms + NORM_EPS)) * g * (1.0 + sc) + sh


def _mod_kernel(c_ref, w_ref, b_ref, o_ref):
    c = c_ref[...]
    ca = _silu(c).astype(BF16)
    o_ref[...] = _dot(ca, w_ref[...].astype(BF16)) + b_ref[...]


def _modulation(c, mod_w, mod_b):
    depth, d, n = mod_w.shape
    b = c.shape[0]
    tn = 1024
    out = pl.pallas_call(
        _mod_kernel,
        grid=(depth, n // tn),
        in_specs=[
            pl.BlockSpec((b, d), lambda l, j: (0, 0)),
            pl.BlockSpec((None, d, tn), lambda l, j: (l, 0, j)),
            pl.BlockSpec((None, 1, tn), lambda l, j: (l, 0, j)),
        ],
        out_specs=pl.BlockSpec((None, b, tn), lambda l, j: (l, 0, j)),
        out_shape=jax.ShapeDtypeStruct((depth, b, n), F32),
        compiler_params=_params(2 * d * tn * 4 + d * tn * 2, 2),
        name="modulation",
    )(c, mod_w, mod_b.reshape(depth, 1, n))
    return out.reshape(depth * b * MOD_PARTS, 1, d)


class _Mod:
    def __init__(self, table, layer, batch, seq):
        self.table = table
        self.base = layer * batch * MOD_PARTS
        self.seq = seq

    def row(self, part, tok0):
        return self.base + (tok0 // self.seq) * MOD_PARTS + part


def _prenorm_kernel(x_ref, g_ref, sc_ref, sh_ref, o_ref):
    o_ref[...] = _adaln(x_ref[...], g_ref[...], sc_ref[...], sh_ref[...]).astype(o_ref.dtype)


def _prenorm(x, gain, mod, part0, tm=512):
    t, d = x.shape
    row = lambda p: pl.BlockSpec((None, 1, d), lambda i: (mod.row(p, i * tm), 0, 0))
    return pl.pallas_call(
        _prenorm_kernel,
        grid=(t // tm,),
        in_specs=[
            pl.BlockSpec((tm, d), lambda i: (i, 0)),
            pl.BlockSpec((1, d), lambda i: (0, 0)),
            row(part0 + 1), row(part0),
        ],
        out_specs=pl.BlockSpec((tm, d), lambda i: (i, 0)),
        out_shape=jax.ShapeDtypeStruct((t, d), BF16),
        compiler_params=_params(2 * tm * d * 6 + 3 * tm * d * 4, 1),
        name="prenorm",
    )(x, gain.reshape(1, d), mod.table, mod.table)


def _final_norm_kernel(x_ref, g_ref, o_ref):
    x = x_ref[...]
    ms = jnp.mean(x * x, axis=-1, keepdims=True)
    o_ref[...] = x * lax.rsqrt(ms + NORM_EPS) * g_ref[...]


def _final_norm(x, gain, tm=512):
    t, d = x.shape
    return pl.pallas_call(
        _final_norm_kernel,
        grid=(t // tm,),
        in_specs=[pl.BlockSpec((tm, d), lambda i: (i, 0)), pl.BlockSpec((1, d), lambda i: (0, 0))],
        out_specs=pl.BlockSpec((tm, d), lambda i: (i, 0)),
        out_shape=jax.ShapeDtypeStruct((t, d), F32),
        compiler_params=_params(6 * tm * d * 4, 1),
        name="final_norm",
    )(x, gain.reshape(1, d))


def _qkv_kernel(x_ref, gn_ref, sc_ref, sh_ref, w_ref, cos_ref, sin_ref, o_ref, wb_ref, *, nq, scale):
    j = pl.program_id(0)
    tm, tn = o_ref.shape

    @pl.when(pl.program_id(1) == 0)
    def _():
        wb_ref[...] = w_ref[...].astype(BF16)

    def tile(mult):
        for r0 in range(0, tm, ROW_CHUNK):
            rs = slice(r0, r0 + ROW_CHUNK)
            h = _adaln(x_ref[rs, :], gn_ref[...], sc_ref[...], sh_ref[...]).astype(BF16)
            acc = _dot(h, wb_ref[...])
            if mult is None:
                o_ref[rs, :] = acc.astype(o_ref.dtype)
                continue
            cos = cos_ref[rs, :]
            sin = sin_ref[rs, :]
            for c in range(tn // HEAD_DIM):
                a = acc[:, c * HEAD_DIM:(c + 1) * HEAD_DIM]
                r = a * cos + pltpu.roll(a, HEAD_DIM // 2, 1) * sin
                if mult != 1.0:
                    r = r * mult
                o_ref[rs, c * HEAD_DIM:(c + 1) * HEAD_DIM] = r.astype(o_ref.dtype)

    @pl.when(j < nq)
    def _():
        tile(scale)

    @pl.when((j >= nq) & (j < 2 * nq))
    def _():
        tile(1.0)

    @pl.when(j >= 2 * nq)
    def _():
        tile(None)


def _qkv_rope(x, gain, mod, w, seq, tm=1024, tn=1024):
    t, d = x.shape
    n = w.shape[1]
    tn = min(tn, d)
    half = HEAD_DIM // 2
    inv = ROPE_THETA ** (-jnp.arange(half, dtype=F32) * (2.0 / HEAD_DIM))
    ang = jnp.arange(seq, dtype=F32)[:, None] * inv[None, :]
    cos = jnp.concatenate([jnp.cos(ang), jnp.cos(ang)], axis=-1)
    sin = jnp.concatenate([-jnp.sin(ang), jnp.sin(ang)], axis=-1)
    tm = min(tm, seq)
    sblk = seq // tm
    kern = functools.partial(_qkv_kernel, nq=d // tn, scale=HEAD_DIM ** -0.5)
    row = lambda p: pl.BlockSpec((None, 1, d), lambda j, i: (mod.row(p, i * tm), 0, 0))
    return pl.pallas_call(
        kern,
        grid=(n // tn, t // tm),
        in_specs=[
            pl.BlockSpec((tm, d), lambda j, i: (i, 0)),
            pl.BlockSpec((1, d), lambda j, i: (0, 0)),
            row(1), row(0),
            pl.BlockSpec((d, tn), lambda j, i: (0, j)),
            pl.BlockSpec((tm, HEAD_DIM), lambda j, i: (i % sblk, 0)),
            pl.BlockSpec((tm, HEAD_DIM), lambda j, i: (i % sblk, 0)),
        ],
        out_specs=pl.BlockSpec((tm, tn), lambda j, i: (i, j)),
        out_shape=jax.ShapeDtypeStruct((t, n), BF16),
        scratch_shapes=[pltpu.VMEM((d, tn), BF16)],
        compiler_params=_params(
            2 * tm * d * 4 + 2 * d * tn * 4 + d * tn * 2 + 2 * tm * tn * 2
            + 4 * ROW_CHUNK * (d + tn) * 4, 2),
        name="qkv_rope",
    )(x, gain.reshape(1, d), mod.table, mod.table, w, cos, sin)


def _moba_kernel(q_ref, k_ref, v_ref, o_ref, kbar_ref, vt_ref, s_ref, p_ref, *, nb, blk, topk):
    kr = kbar_ref.shape[0]

    kbar_ref[...] = jnp.zeros(kbar_ref.shape, F32)
    for m in range(nb):
        rows = slice(m * blk, (m + 1) * blk)
        kbar_ref[m:m + 1, :] = jnp.mean(k_ref[rows, :].astype(F32), axis=0, keepdims=True)
        vt_ref[:, rows] = v_ref[rows, :].astype(F32).T.astype(BF16)
    kbar = kbar_ref[...]
    kb_hi = kbar.astype(BF16)
    kb_lo = (kbar - kb_hi.astype(F32)).astype(BF16)

    blk_id = lax.broadcasted_iota(I32, (kr, blk), 0)
    causal = lax.broadcasted_iota(I32, (blk, blk), 0) <= lax.broadcasted_iota(I32, (blk, blk), 1)

    for n in range(nb):
        q = q_ref[n * blk:(n + 1) * blk, :]
        if n > 0:
            gate = _dot_nt(kb_hi, q) + _dot_nt(kb_lo, q)
            valid = blk_id < n
            g = jnp.where(valid, gate, NEG_INF)
            cnt = jnp.zeros((kr, blk), F32)
            for mp in range(n):
                other = g[mp:mp + 1, :]
                cnt = cnt + jnp.where(other > g, 1.0, jnp.where((other == g) & (blk_id > mp), 1.0, 0.0))
            sel = jnp.where(valid & (cnt < topk), 1.0, 0.0)

        mx = None
        for m in range(n + 1):
            rows = slice(m * blk, (m + 1) * blk)
            s = _dot_nt(k_ref[rows, :], q)
            s = jnp.where(causal if m == n else sel[m:m + 1, :] > 0.0, s, NEG_INF)
            s_ref[rows, :] = s
            bm = jnp.max(s, axis=0, keepdims=True)
            mx = bm if mx is None else jnp.maximum(mx, bm)
        denom = jnp.zeros((1, blk), F32)
        for m in range(n + 1):
            rows = slice(m * blk, (m + 1) * blk)
            p = jnp.exp(s_ref[rows, :] - mx)
            denom = denom + jnp.sum(p, axis=0, keepdims=True)
            p_ref[rows, :] = p.astype(BF16)
        keys = (n + 1) * blk
        acc = _dot(vt_ref[:, 0:keys], p_ref[0:keys, :])
        o_ref[n * blk:(n + 1) * blk, :] = (acc / denom).T.astype(o_ref.dtype)


def _moba_attention(qkv, batch, seq, d):
    t = qkv.shape[0]
    heads = d // HEAD_DIM
    blk = MOBA_BLOCK
    assert seq % blk == 0
    nb = seq // blk
    kr = 16
    assert nb <= kr
    kern = functools.partial(_moba_kernel, nb=nb, blk=blk, topk=min(MOBA_TOPK, nb - 1))
    return pl.pallas_call(
        kern,
        grid=(batch, heads),
        in_specs=[
            pl.BlockSpec((seq, HEAD_DIM), lambda b, h: (b, h)),
            pl.BlockSpec((seq, HEAD_DIM), lambda b, h: (b, heads + h)),
            pl.BlockSpec((seq, HEAD_DIM), lambda b, h: (b, 2 * heads + h)),
        ],
        out_specs=pl.BlockSpec((seq, HEAD_DIM), lambda b, h: (b, h)),
        out_shape=jax.ShapeDtypeStruct((t, d), BF16),
        scratch_shapes=[pltpu.VMEM((kr, HEAD_DIM), F32), pltpu.VMEM((HEAD_DIM, seq), BF16),
                        pltpu.VMEM((seq, blk), F32), pltpu.VMEM((seq, blk), BF16)],
        compiler_params=_params(10 * seq * HEAD_DIM * 2 + seq * blk * 6 + 16 * blk * blk * 4, 2),
        name="moba_attention",
    )(qkv, qkv, qkv)


def _proj_res_kernel(*refs, has_bias):
    if has_bias:
        a_ref, w_ref, r_ref, g_ref, b_ref, o_ref, wb_ref = refs
    else:
        a_ref, w_ref, r_ref, g_ref, o_ref, wb_ref = refs

    @pl.when(pl.program_id(1) == 0)
    def _():
        wb_ref[...] = w_ref[...].astype(BF16)

    y = _dot(a_ref[...], wb_ref[...])
    if has_bias:
        y = y + b_ref[...]
    o_ref[...] = r_ref[...] + g_ref[...] * y


def _proj_residual(a, w3, widx, resid, mod, gate_part, bias=None, tm=1024, tn=512):
    t, k = a.shape
    n = w3.shape[2]
    in_specs = [
        pl.BlockSpec((tm, k), lambda j, i: (i, 0)),
        pl.BlockSpec((None, k, tn), lambda j, i: (widx, 0, j)),
        pl.BlockSpec((tm, tn), lambda j, i: (i, j)),
        pl.BlockSpec((None, 1, tn), lambda j, i: (mod.row(gate_part, i * tm), 0, j)),
    ]
    args = [a, w3, resid, mod.table]
    if bias is not None:
        in_specs.append(pl.BlockSpec((1, tn), lambda j, i: (0, j)))
        args.append(bias.reshape(1, n))
    return pl.pallas_call(
        functools.partial(_proj_res_kernel, has_bias=bias is not None),
        grid=(n // tn, t // tm),
        in_specs=in_specs,
        out_specs=pl.BlockSpec((tm, tn), lambda j, i: (i, j)),
        out_shape=jax.ShapeDtypeStruct((t, n), F32),
        scratch_shapes=[pltpu.VMEM((k, tn), BF16)],
        compiler_params=_params(
            2 * tm * k * 2 + 2 * k * tn * 4 + k * tn * 2 + 5 * tm * tn * 4, 2),
        name="proj_residual",
    )(*args)


def _shift_rows(p, halo, k):
    hr = halo.shape[0]
    body = pltpu.roll(p, k, 0)
    head = jnp.where(lax.broadcasted_iota(I32, (hr, p.shape[1]), 0) < k,
                     pltpu.roll(halo, k, 0), body[:hr])
    return head, body


def _shortconv_kernel(x_ref, wb_g, wc_g, wx_g, cw_ref, o_ref, wb_ref, halo_ref, *, width, tiles_per_seq):
    i = pl.program_id(1)

    @pl.when(i == 0)
    def _():
        wb_ref[0] = wb_g[...].astype(BF16)
        wb_ref[1] = wc_g[...].astype(BF16)
        wb_ref[2] = wx_g[...].astype(BF16)

    @pl.when(i % tiles_per_seq == 0)
    def _():
        halo_ref[...] = jnp.zeros(halo_ref.shape, F32)

    x = x_ref[...]
    gb = _dot(x, wb_ref[0])
    p = _dot(x, wb_ref[1]) * _dot(x, wb_ref[2])
    hr = halo_ref.shape[0]
    halo = halo_ref[...]
    cw = cw_ref[...]
    body = cw[width - 1:width, :] * p
    head = body[:hr]
    for s in range(1, width):
        hs, bs = _shift_rows(p, halo, s)
        w = cw[width - 1 - s:width - s, :]
        body = body + w * bs
        head = head + w * hs
    o_ref[...] = (gb * body).astype(o_ref.dtype)
    o_ref[0:hr, :] = (gb[:hr] * head).astype(o_ref.dtype)
    halo_ref[...] = p[p.shape[0] - hr:, :]


def _short_conv_in(h, w_in, conv_w, seq, tm=1024, tn=256):
    t, d = h.shape
    width = conv_w.shape[0]
    nd = d // tn
    hr = 16
    assert width - 1 <= hr
    kern = functools.partial(_shortconv_kernel, width=width, tiles_per_seq=seq // tm)
    wspec = lambda off: pl.BlockSpec((d, tn), lambda j, i: (0, off * nd + j))
    return pl.pallas_call(
        kern,
        grid=(nd, t // tm),
        in_specs=[
            pl.BlockSpec((tm, d), lambda j, i: (i, 0)),
            wspec(0), wspec(1), wspec(2),
            pl.BlockSpec((width, tn), lambda j, i: (0, j)),
        ],
        out_specs=pl.BlockSpec((tm, tn), lambda j, i: (i, j)),
        out_shape=jax.ShapeDtypeStruct((t, d), BF16),
        scratch_shapes=[pltpu.VMEM((3, d, tn), BF16), pltpu.VMEM((hr, tn), F32)],
        compiler_params=_params(
            2 * tm * d * 2 + 6 * d * tn * 4 + 3 * d * tn * 2 + 10 * tm * tn * 4, 2),
        name="short_conv_in",
    )(h, w_in, w_in, w_in, conv_w)


def _pool_kernel(x_ref, xh_ref, gn_ref, sc_ref, sh_ref, w_ref, ps_ref, ga_ref, o_ref,
                 wb_ref, ext_ref, *, windows, tiles_per_seq):
    i = pl.program_id(0)
    tm, d = x_ref.shape
    hr = xh_ref.shape[0]
    cg = d // len(windows)

    @pl.when(i == 0)
    def _():
        wb_ref[...] = w_ref[...].astype(BF16)

    x = x_ref[...]
    gn, sc, sh = gn_ref[...], sc_ref[...], sh_ref[...]
    h = _adaln(x, gn, sc, sh)
    hh = _adaln(xh_ref[...], gn, sc, sh)
    first = i % tiles_per_seq == 0
    ext_ref[0:hr, :] = jnp.where(first, 0.0, hh)
    ext_ref[hr:, :] = h
    pos = (i % tiles_per_seq) * tm + lax.broadcasted_iota(I32, (tm, 1), 0)
    for g, w in enumerate(windows):
        cols = slice(g * cg, (g + 1) * cg)
        s = ext_ref[:, cols]
        step = 1
        while step < w:
            s = s + pltpu.roll(s, step, 0)
            step *= 2
        count = jnp.minimum(pos + 1, w).astype(F32)
        pooled = s[hr:] / count - h[:, cols]
        mixed = _dot(pooled.astype(BF16), wb_ref[g])
        o_ref[:, cols] = x[:, cols] + ga_ref[:, cols] * (mixed * ps_ref[:, cols])


def _pool_mixer(x, gain, mod, w_group, scale, seq, tm=256):
    t, d = x.shape
    g, cg, _ = w_group.shape
    assert g == len(POOL_WINDOWS) and cg % LANES == 0
    hr = 16
    assert max(POOL_WINDOWS) <= hr and all(w & (w - 1) == 0 for w in POOL_WINDOWS)
    kern = functools.partial(_pool_kernel, windows=POOL_WINDOWS, tiles_per_seq=seq // tm)
    row = lambda p: pl.BlockSpec((None, 1, d), lambda i: (mod.row(p, i * tm), 0, 0))
    return pl.pallas_call(
        kern,
        grid=(t // tm,),
        in_specs=[
            pl.BlockSpec((tm, d), lambda i: (i, 0)),
            pl.BlockSpec((hr, d), lambda i: (jnp.maximum(i * (tm // hr) - 1, 0), 0)),
            pl.BlockSpec((1, d), lambda i: (0, 0)),
            row(1), row(0),
            pl.BlockSpec((g, cg, cg), lambda i: (0, 0, 0)),
            pl.BlockSpec((1, d), lambda i: (0, 0)),
            row(2),
        ],
        out_specs=pl.BlockSpec((tm, d), lambda i: (i, 0)),
        out_shape=jax.ShapeDtypeStruct((t, d), F32),
        scratch_shapes=[pltpu.VMEM((g, cg, cg), BF16), pltpu.VMEM((tm + hr, d), F32)],
        compiler_params=_params(4 * tm * d * 4 + 2 * g * cg * cg * 4 + g * cg * cg * 2
                                + 6 * tm * d * 4, 1),
        name="pool_mixer",
    )(x, x, gain.reshape(1, d), mod.table, mod.table, w_group, scale.reshape(1, d), mod.table)


def _glu_kernel(x_ref, gn_ref, sc_ref, sh_ref, wa_ref, wg_ref, ba_ref, bg_ref, o_ref, wb_ref):
    @pl.when(pl.program_id(1) == 0)
    def _():
        wb_ref[0] = wa_ref[...].astype(BF16)
        wb_ref[1] = wg_ref[...].astype(BF16)

    for r0 in range(0, x_ref.shape[0], ROW_CHUNK):
        rs = slice(r0, r0 + ROW_CHUNK)
        h = _adaln(x_ref[rs, :], gn_ref[...], sc_ref[...], sh_ref[...]).astype(BF16)
        a = _dot(h, wb_ref[0]) + ba_ref[...]
        g = _dot(h, wb_ref[1]) + bg_ref[...]
        o_ref[rs, :] = a * jax.nn.sigmoid(g)


def _conformer_glu(x, gain, mod, w1, b1, tm=1024, tn=512):
    t, d = x.shape
    n = w1.shape[1] // 2
    nd = n // tn
    b1 = b1.reshape(1, 2 * n)
    row = lambda p: pl.BlockSpec((None, 1, d), lambda j, i: (mod.row(p, i * tm), 0, 0))
    return pl.pallas_call(
        _glu_kernel,
        grid=(nd, t // tm),
        in_specs=[
            pl.BlockSpec((tm, d), lambda j, i: (i, 0)),
            pl.BlockSpec((1, d), lambda j, i: (0, 0)),
            row(1), row(0),
            pl.BlockSpec((d, tn), lambda j, i: (0, j)),
            pl.BlockSpec((d, tn), lambda j, i: (0, nd + j)),
            pl.BlockSpec((1, tn), lambda j, i: (0, j)),
            pl.BlockSpec((1, tn), lambda j, i: (0, nd + j)),
        ],
        out_specs=pl.BlockSpec((tm, tn), lambda j, i: (i, j)),
        out_shape=jax.ShapeDtypeStruct((t, n), F32),
        scratch_shapes=[pltpu.VMEM((2, d, tn), BF16)],
        compiler_params=_params(
            2 * tm * d * 4 + 4 * d * tn * 4 + 2 * d * tn * 2 + 2 * tm * tn * 4
            + 4 * ROW_CHUNK * (d + tn) * 4, 2),
        name="conformer_glu",
    )(x, gain.reshape(1, d), mod.table, mod.table, w1, w1, b1, b1)


def _cfconv_kernel(u_ref, uh_ref, dw_ref, dwb_ref, lg_ref, lb_ref, o_ref, ph_ref, y_ref,
                   *, width, tiles_per_seq, rc, cc):
    i = pl.program_id(0)
    tm, d = u_ref.shape
    hr = uh_ref.shape[0]
    rows = tm + hr
    first = i % tiles_per_seq == 0
    off = hr - (width - 1)
    for c0 in range(0, d, cc):
        cols = slice(c0, c0 + cc)
        ph_ref[0, 0:hr, :] = jnp.where(first, 0.0, uh_ref[:, cols])
        ph_ref[0, hr:, :] = u_ref[:, cols]
        e0 = ph_ref[0]
        for s in range(1, SUBLANES):
            ph_ref[s] = pltpu.roll(e0, rows - s, 0)
        w = dw_ref[:, cols]
        for r0 in range(0, tm, rc):
            acc = jnp.zeros((rc, cc), F32) + dwb_ref[:, cols]
            for k in range(width):
                q, s = divmod(off + k, SUBLANES)
                a0 = r0 + q * SUBLANES
                acc = acc + w[k:k + 1, :] * ph_ref[s, a0:a0 + rc, :]
            y_ref[r0:r0 + rc, cols] = acc
    y = y_ref[...]
    mu = jnp.mean(y, axis=-1, keepdims=True)
    yc = y - mu
    var = jnp.mean(yc * yc, axis=-1, keepdims=True)
    z = yc * lax.rsqrt(var + LN_EPS) * lg_ref[...] + lb_ref[...]
    o_ref[...] = _silu(z).astype(o_ref.dtype)


def _conformer_conv(u, dw, dw_b, ln_g, ln_b, seq, tm=256):
    t, d = u.shape
    width = dw.shape[0]
    hr = 32
    assert width - 1 <= hr
    cc = min(256, d)
    kern = functools.partial(_cfconv_kernel, width=width, tiles_per_seq=seq // tm,
                             rc=min(64, tm), cc=cc)
    vec = lambda: pl.BlockSpec((1, d), lambda i: (0, 0))
    return pl.pallas_call(
        kern,
        grid=(t // tm,),
        in_specs=[
            pl.BlockSpec((tm, d), lambda i: (i, 0)),
            pl.BlockSpec((hr, d), lambda i: (jnp.maximum(i * (tm // hr) - 1, 0), 0)),
            pl.BlockSpec((width, d), lambda i: (0, 0)),
            vec(), vec(), vec(),
        ],
        out_specs=pl.BlockSpec((tm, d), lambda i: (i, 0)),
        out_shape=jax.ShapeDtypeStruct((t, d), BF16),
        scratch_shapes=[pltpu.VMEM((SUBLANES, tm + hr, cc), F32), pltpu.VMEM((tm, d), F32)],
        compiler_params=_params(8 * tm * d * 4 + SUBLANES * (tm + hr) * cc * 4, 1),
        name="conformer_conv",
    )(u, u, dw, dw_b.reshape(1, d), ln_g.reshape(1, d), ln_b.reshape(1, d))


def _pick_cols(n, k, must_divide):
    best = None
    for c in range(MXU_COLS, n + 1, MXU_COLS):
        if k * c * 4 <= STAGE_SLAB_BYTES and (n % c == 0 or not must_divide):
            best = c
    assert best is not None
    return best


def _by_block(jj, n_blocks, full, last, fn):
    if last == full:
        fn(full)
    else:
        jj = jnp.asarray(jj, I32)
        pl.when(jj < n_blocks - 1)(lambda: fn(full))
        pl.when(jj == n_blocks - 1)(lambda: fn(last))


def _stage_weights(te_ref, nt_ref, nxt_ref, slabs, stage_ref, wb_ref, sem, last_cols):
    j, i = pl.program_id(0), pl.program_id(1)
    n_blocks = pl.num_programs(0)
    k, tn = stage_ref.shape[1], stage_ref.shape[2]

    def each_copy(e, jj, op):
        def run(width):
            for s, (w, col) in enumerate(slabs):
                op(pltpu.make_async_copy(
                    w.at[e, :, pl.ds(pl.multiple_of(col(jj) * tn, LANES), width)],
                    stage_ref.at[s, :, pl.ds(0, width)], sem.at[s]))

        _by_block(jj, n_blocks, tn, last_cols, run)

    start = lambda c: c.start()
    new_w = (i == 0) | (te_ref[i] != te_ref[jnp.maximum(i - 1, 0)])

    @pl.when(new_w)
    def _():
        @pl.when((i == 0) & (j == 0))
        def _():
            each_copy(te_ref[0], 0, start)

        each_copy(te_ref[i], j, lambda c: c.wait())

        def cast(c, carry):
            r = pl.multiple_of(c * CAST_ROWS, CAST_ROWS)
            for s in range(len(slabs)):
                wb_ref[s, pl.ds(r, CAST_ROWS), :] = stage_ref[s, pl.ds(r, CAST_ROWS), :].astype(BF16)
            return carry

        lax.fori_loop(0, k // CAST_ROWS, cast, 0)

        nxt = nxt_ref[i]
        more = nxt < nt_ref[0]

        @pl.when(more)
        def _():
            each_copy(te_ref[nxt], j, start)

        @pl.when(jnp.logical_not(more) & (j + 1 < n_blocks))
        def _():
            each_copy(te_ref[0], j + 1, start)


def _gateup_kernel(te_ref, nt_ref, nxt_ref, rv_ref, x_ref, gn_ref, sc_ref, sh_ref, wg_hbm, wu_hbm,
                   o_ref, stage_ref, wb_ref, sem, *, norm, last_cols):
    j, i = pl.program_id(0), pl.program_id(1)
    same = lambda jj: jj
    _stage_weights(te_ref, nt_ref, nxt_ref, [(wg_hbm, same), (wu_hbm, same)], stage_ref, wb_ref, sem,
                   last_cols)
    valid = rv_ref[i]

    def tile(width):
        for r0 in range(0, x_ref.shape[0], ROW_CHUNK):
            rs = slice(r0, r0 + ROW_CHUNK)

            @pl.when(r0 < valid)
            def _():
                x = x_ref[rs, :]
                if norm:
                    x = _adaln(x, gn_ref[...], sc_ref[...], sh_ref[...]).astype(BF16)
                g = _dot(x, wb_ref[0, :, :width])
                u = _dot(x, wb_ref[1, :, :width])
                o_ref[rs, :width] = (_silu(g) * u).astype(o_ref.dtype)

            @pl.when(r0 >= valid)
            def _():
                o_ref[rs, :width] = jnp.zeros((ROW_CHUNK, width), o_ref.dtype)

    _by_block(j, pl.num_programs(0), o_ref.shape[1], last_cols, tile)


def _grouped_gateup(xs, wg, wu, tile_expert, n_tiles, next_group, rows_valid, tm, gain, mod, norm):
    p, d = xs.shape
    f = wg.shape[2]
    assert d % CAST_ROWS == 0 and tm % ROW_CHUNK == 0
    tf = _pick_cols(f, d, must_divide=False)
    n_blocks = pl.cdiv(f, tf)
    last_cols = f - (n_blocks - 1) * tf
    row = lambda part: pl.BlockSpec(
        (None, 1, d), lambda j, i, te, nt, nx, rv: (mod.row(part, i * tm if norm else 0), 0, 0))
    return pl.pallas_call(
        functools.partial(_gateup_kernel, norm=norm, last_cols=last_cols),
        grid_spec=pltpu.PrefetchScalarGridSpec(
            num_scalar_prefetch=4,
            grid=(n_blocks, p // tm),
            in_specs=[
                pl.BlockSpec((tm, d), lambda j, i, te, nt, nx, rv: (jnp.minimum(i, nt[0] - 1), 0)),
                pl.BlockSpec((1, d), lambda j, i, te, nt, nx, rv: (0, 0)),
                row(4), row(3),
                pl.BlockSpec(memory_space=pltpu.HBM),
                pl.BlockSpec(memory_space=pltpu.HBM),
            ],
            out_specs=pl.BlockSpec((tm, tf), lambda j, i, te, nt, nx, rv: (i, j)),
            scratch_shapes=[pltpu.VMEM((2, d, tf), F32), pltpu.VMEM((2, d, tf), BF16),
                            pltpu.SemaphoreType.DMA((2,))],
        ),
        out_shape=jax.ShapeDtypeStruct((p, f), BF16),
        compiler_params=_params(
            2 * d * tf * 6 + 2 * tm * d * xs.dtype.itemsize + 2 * tm * tf * 2
            + 4 * ROW_CHUNK * (tf + d) * 4, 2),
        name="grouped_gateup",
    )(tile_expert, n_tiles, next_group, rows_valid, xs, gain.reshape(1, d), mod.table, mod.table, wg, wu)


def _pack_pair(lo, hi):
    bl = lax.bitcast_convert_type(lo.astype(BF16).astype(F32), U32) >> 16
    bh = lax.bitcast_convert_type(hi.astype(BF16).astype(F32), U32) & jnp.uint32(0xFFFF0000)
    return bh | bl


def _unpack_pair(w):
    lo = lax.bitcast_convert_type(w << 16, F32)
    hi = lax.bitcast_convert_type(w & jnp.uint32(0xFFFF0000), F32)
    return lo, hi


def _down_kernel(te_ref, nt_ref, nxt_ref, rv_ref, a_ref, w_hbm, o_ref, stage_ref, wb_ref, sem, *, half_blocks):
    i = pl.program_id(1)
    slabs = [(w_hbm, lambda jj: jj), (w_hbm, lambda jj: half_blocks + jj)]
    _stage_weights(te_ref, nt_ref, nxt_ref, slabs, stage_ref, wb_ref, sem, stage_ref.shape[2])
    valid = rv_ref[i]
    for r0 in range(0, a_ref.shape[0], ROW_CHUNK):
        rs = slice(r0, r0 + ROW_CHUNK)

        @pl.when(r0 < valid)
        def _():
            a = a_ref[rs, :]
            o_ref[rs, :] = _pack_pair(_dot(a, wb_ref[0]), _dot(a, wb_ref[1]))

        @pl.when(r0 >= valid)
        def _():
            o_ref[rs, :] = jnp.zeros((ROW_CHUNK, o_ref.shape[1]), o_ref.dtype)


def _grouped_down(act, wd, tile_expert, n_tiles, next_group, rows_valid, tm):
    p, f = act.shape
    n = wd.shape[2]
    assert f % CAST_ROWS == 0 and tm % ROW_CHUNK == 0
    tn = _pick_cols(n // 2, f, must_divide=True)
    half_blocks = (n // 2) // tn
    return pl.pallas_call(
        functools.partial(_down_kernel, half_blocks=half_blocks),
        grid_spec=pltpu.PrefetchScalarGridSpec(
            num_scalar_prefetch=4,
            grid=(half_blocks, p // tm),
            in_specs=[
                pl.BlockSpec((tm, f), lambda j, i, te, nt, nx, rv: (jnp.minimum(i, nt[0] - 1), 0)),
                pl.BlockSpec(memory_space=pltpu.HBM),
            ],
            out_specs=pl.BlockSpec((tm, tn), lambda j, i, te, nt, nx, rv: (i, j)),
            scratch_shapes=[pltpu.VMEM((2, f, tn), F32), pltpu.VMEM((2, f, tn), BF16),
                            pltpu.SemaphoreType.DMA((2,))],
        ),
        out_shape=jax.ShapeDtypeStruct((p, n // 2), U32),
        compiler_params=_params(2 * f * tn * 6 + 2 * tm * f * 2 + 2 * tm * tn * 4 + 8 * ROW_CHUNK * tn * 4, 2),
        name="grouped_down",
    )(tile_expert, n_tiles, next_group, rows_valid, act, wd)


def _dense_ffn(x, gain, wg, wu, wd, widx, mod, tm=512):
    t = x.shape[0]
    nt = t // tm
    te = jnp.full((nt,), widx, I32)
    single_group = jnp.full((nt,), nt, I32)
    all_rows = jnp.full((nt,), tm, I32)
    act = _grouped_gateup(x, wg, wu, te, jnp.full((1,), nt, I32), single_group, all_rows, tm,
                          gain, mod, norm=True)
    return _proj_residual(act, wd, widx, x, mod, 5, tm=tm, tn=512)


def _router_kernel(x_ref, gn_ref, sc_ref, sh_ref, r_ref, h_ref, info_ref, cnt_ref, run_ref, *, n_exp):
    i = pl.program_id(0)
    tm = x_ref.shape[0]

    @pl.when(i == 0)
    def _():
        run_ref[...] = jnp.zeros(run_ref.shape, F32)

    h = _adaln(x_ref[...], gn_ref[...], sc_ref[...], sh_ref[...])
    half = h.shape[1] // 2
    h_ref[...] = _pack_pair(h[:, :half], h[:, half:])
    h_hi = h.astype(BF16)
    h_lo = (h - h_hi.astype(F32)).astype(BF16)
    r = r_ref[...]
    r_hi = r.astype(BF16)
    r_lo = (r - r_hi.astype(F32)).astype(BF16)
    logits = _dot(h_hi, r_hi) + (_dot(h_lo, r_hi) + _dot(h_hi, r_lo))

    lane = lax.broadcasted_iota(I32, (tm, LANES), 1).astype(F32)
    lg = jnp.where(lane < n_exp, logits, -jnp.inf)
    v1 = jnp.max(lg, axis=1, keepdims=True)
    i1 = jnp.min(jnp.where(lg == v1, lane, float(LANES)), axis=1, keepdims=True)
    lg2 = jnp.where(lane == i1, -jnp.inf, lg)
    v2 = jnp.max(lg2, axis=1, keepdims=True)
    i2 = jnp.min(jnp.where(lg2 == v2, lane, float(LANES)), axis=1, keepdims=True)
    e = jnp.exp(v2 - v1)
    g1 = 1.0 / (1.0 + e)
    g2 = e / (1.0 + e)

    oh = jnp.where((lane == i1) | (lane == i2), 1.0, 0.0)
    tri = jnp.where(lax.broadcasted_iota(I32, (tm, tm), 0) > lax.broadcasted_iota(I32, (tm, tm), 1),
                    1.0, 0.0).astype(BF16)
    cum = _dot(tri, oh.astype(BF16)) + run_ref[...]
    rank1 = jnp.sum(jnp.where(lane == i1, cum, 0.0), axis=1, keepdims=True)
    rank2 = jnp.sum(jnp.where(lane == i2, cum, 0.0), axis=1, keepdims=True)
    run = run_ref[...] + jnp.sum(oh, axis=0, keepdims=True)
    run_ref[...] = run
    cnt_ref[...] = jnp.broadcast_to(run, cnt_ref.shape)

    info = jnp.zeros((tm, LANES), F32)
    for k, val in enumerate((i1, i2, g1, g2, rank1, rank2)):
        info = jnp.where(lane == float(k), val, info)
    info_ref[...] = info


def _route(x, gain, mod, router, tm=256):
    t, d = x.shape
    n_exp = router.shape[1]
    assert n_exp <= LANES
    rpad = jnp.pad(router, ((0, 0), (0, LANES - n_exp)))
    row = lambda p: pl.BlockSpec((None, 1, d), lambda i: (mod.row(p, i * tm), 0, 0))
    return pl.pallas_call(
        functools.partial(_router_kernel, n_exp=n_exp),
        grid=(t // tm,),
        in_specs=[
            pl.BlockSpec((tm, d), lambda i: (i, 0)),
            pl.BlockSpec((1, d), lambda i: (0, 0)),
            row(4), row(3),
            pl.BlockSpec((d, LANES), lambda i: (0, 0)),
        ],
        out_specs=[
            pl.BlockSpec((tm, d // 2), lambda i: (i, 0)),
            pl.BlockSpec((tm, LANES), lambda i: (i, 0)),
            pl.BlockSpec((8, LANES), lambda i: (0, 0)),
        ],
        out_shape=[
            jax.ShapeDtypeStruct((t, d // 2), U32),
            jax.ShapeDtypeStruct((t, LANES), F32),
            jax.ShapeDtypeStruct((8, LANES), F32),
        ],
        scratch_shapes=[pltpu.VMEM((1, LANES), F32)],
        compiler_params=_params(4 * tm * d * 4 + 4 * tm * d * 4 + 2 * d * LANES * 4, 1),
        name="moe_route",
    )(x, gain.reshape(1, d), mod.table, mod.table, rpad)


ROW_DMA_UNROLL = 8


def _gather_kernel(tok_ref, tot_ref, hp_ref, o_ref, buf_ref):
    i = pl.program_id(0)
    tg, half = buf_ref.shape
    base = i * tg

    @pl.when(base < tot_ref[0])
    def _():
        def copy_row(r, c):
            buf_ref[pl.ds(r, 1), :] = hp_ref[pl.ds(tok_ref[base + r], 1), :]
            return c

        lax.fori_loop(0, tg, copy_row, 0, unroll=ROW_DMA_UNROLL)
        lo, hi = _unpack_pair(buf_ref[...])
        o_ref[:, :half] = lo.astype(o_ref.dtype)
        o_ref[:, half:] = hi.astype(o_ref.dtype)

    @pl.when(base >= tot_ref[0])
    def _():
        o_ref[...] = jnp.zeros(o_ref.shape, o_ref.dtype)


def _gather_rows(hp, sorted_tok, total_rows, tg=256):
    t, half = hp.shape
    p = sorted_tok.shape[0]
    return pl.pallas_call(
        _gather_kernel,
        grid_spec=pltpu.PrefetchScalarGridSpec(
            num_scalar_prefetch=2,
            grid=(p // tg,),
            in_specs=[pl.BlockSpec(memory_space=pltpu.VMEM)],
            out_specs=pl.BlockSpec((tg, 2 * half), lambda i, tok, tot: (i, 0)),
            scratch_shapes=[pltpu.VMEM((tg, half), U32)],
        ),
        out_shape=jax.ShapeDtypeStruct((p, 2 * half), BF16),
        compiler_params=_params(t * half * 4 + 8 * tg * half * 4, 1),
        name="moe_gather",
    )(sorted_tok, total_rows, hp)


def _combine_kernel(*refs, final):
    if final:
        pos_ref, y_hbm, x_ref, info_ref, gf_ref, nf_ref, o_ref, buf_ref, sem = refs
    else:
        pos_ref, y_hbm, x_ref, info_ref, gf_ref, o_ref, buf_ref, sem = refs
    i = pl.program_id(0)
    tc, half = buf_ref.shape[2], buf_ref.shape[3]

    def row_copy(slot, k, r, src_row):
        return pltpu.make_async_copy(
            y_hbm.at[pl.ds(src_row, 1)], buf_ref.at[slot, k, pl.ds(r, 1)], sem.at[slot])

    def fetch(tile):
        base = tile * tc

        def start(r, c):
            for k in range(MOE_TOPK):
                row_copy(tile % 2, k, r, pos_ref[MOE_TOPK * (base + r) + k]).start()
            return c

        lax.fori_loop(0, tc, start, 0, unroll=ROW_DMA_UNROLL)

    @pl.when(i == 0)
    def _():
        fetch(i)

    @pl.when(i + 1 < pl.num_programs(0))
    def _():
        fetch(i + 1)

    def wait(r, c):
        for k in range(MOE_TOPK):
            row_copy(i % 2, k, r, 0).wait()
        return c

    lax.fori_loop(0, tc, wait, 0, unroll=ROW_DMA_UNROLL)
    info = info_ref[...]
    y_lo = jnp.zeros((tc, half), F32)
    y_hi = jnp.zeros((tc, half), F32)
    for k in range(MOE_TOPK):
        lo, hi = _unpack_pair(buf_ref[i % 2, k])
        gate = info[:, MOE_TOPK + k:MOE_TOPK + k + 1]
        y_lo = y_lo + gate * lo
        y_hi = y_hi + gate * hi
    xn_lo = x_ref[:, :half] + gf_ref[:, :half] * y_lo
    xn_hi = x_ref[:, half:] + gf_ref[:, half:] * y_hi
    if final:
        ssq = jnp.sum(xn_lo * xn_lo, axis=-1, keepdims=True) + jnp.sum(xn_hi * xn_hi, axis=-1, keepdims=True)
        inv = lax.rsqrt(ssq / (2 * half) + NORM_EPS)
        xn_lo = xn_lo * inv * nf_ref[:, :half]
        xn_hi = xn_hi * inv * nf_ref[:, half:]
    o_ref[:, :half] = xn_lo
    o_ref[:, half:] = xn_hi


def _combine(y, pos_flat, x, info, mod, final_gain=None, tc=256):
    t, d = x.shape
    final = final_gain is not None
    in_specs = [
        pl.BlockSpec(memory_space=pl.ANY),
        pl.BlockSpec((tc, d), lambda i, pos: (i, 0)),
        pl.BlockSpec((tc, LANES), lambda i, pos: (i, 0)),
        pl.BlockSpec((None, 1, d), lambda i, pos: (mod.row(5, i * tc), 0, 0)),
    ]
    args = [pos_flat, y, x, info, mod.table]
    if final:
        in_specs.append(pl.BlockSpec((1, d), lambda i, pos: (0, 0)))
        args.append(final_gain.reshape(1, d))
    return pl.pallas_call(
        functools.partial(_combine_kernel, final=final),
        grid_spec=pltpu.PrefetchScalarGridSpec(
            num_scalar_prefetch=1,
            grid=(t // tc,),
            in_specs=in_specs,
            out_specs=pl.BlockSpec((tc, d), lambda i, pos: (i, 0)),
            scratch_shapes=[pltpu.VMEM((2, MOE_TOPK, tc, d // 2), U32),
                            pltpu.SemaphoreType.DMA((2,))],
        ),
        out_shape=jax.ShapeDtypeStruct((t, d), F32),
        compiler_params=_params(2 * tc * d * 4 + 4 * tc * d * 4 + 4 * tc * d * 4, 1),
        name="moe_combine",
    )(*args)


def _moe_ffn(x, gain, mod, router, wg, wu, wd, widx, final_gain=None, tm=512):
    t, d = x.shape
    n_exp = router.shape[1]
    wg, wu, wd = (w.reshape((-1,) + w.shape[2:]) for w in (wg, wu, wd))
    hf, info, cnt = _route(x, gain, mod, router)

    idx = info[:, 0:2].astype(I32)
    rank = info[:, 4:6].astype(I32)
    counts = cnt[0, :n_exp].astype(I32)
    padded = ((counts + tm - 1) // tm) * tm
    ends = jnp.cumsum(padded)
    starts = ends - padded
    pos = starts[idx] + rank
    p_max = MOE_TOPK * t + n_exp * tm
    tok = jnp.repeat(jnp.arange(t, dtype=I32), MOE_TOPK)
    sorted_tok = jnp.zeros((p_max,), I32).at[pos.reshape(-1)].set(
        tok, unique_indices=True, mode="promise_in_bounds")
    total = ends[-1:].astype(I32)
    n_tiles = total // tm
    tile_row = jnp.minimum(jnp.arange(p_max // tm, dtype=I32), n_tiles[0] - 1) * tm
    local_expert = jnp.minimum(
        jnp.sum((ends[None, :] <= tile_row[:, None]).astype(I32), axis=1), n_exp - 1)
    tile_expert = widx * n_exp + local_expert
    next_group = (ends // tm)[local_expert]
    tile_ids = jnp.arange(p_max // tm, dtype=I32)
    rows_valid = jnp.where(
        tile_ids < n_tiles[0],
        jnp.clip((starts + counts)[local_expert] - tile_ids * tm, 0, tm), 0).astype(I32)

    xs = _gather_rows(hf, sorted_tok, total)
    act = _grouped_gateup(xs, wg, wu, tile_expert, n_tiles, next_group, rows_valid, tm,
                          gain, mod, norm=False)
    y = _grouped_down(act, wd, tile_expert, n_tiles, next_group, rows_valid, tm)
    return _combine(y, pos.reshape(-1), x, info, mod, final_gain)


def kernel(x, c, mod_w, mod_b, norm_mix, norm_ffn, norm_final, moba_wqkv, moba_wo, sc_w_in, sc_conv,
           sc_w_out, pool_w, pool_scale, cf_w1, cf_b1, cf_dw, cf_dw_b, cf_ln_g, cf_ln_b, cf_w2, cf_b2,
           ffn_wg, ffn_wu, ffn_wd, moe_router, moe_wg, moe_wu, moe_wd):
    batch, seq, d = x.shape
    depth = mod_w.shape[0]
    table = _modulation(c, mod_w, mod_b)
    xt = x.reshape(batch * seq, d)
    for layer in range(depth):
        mod = _Mod(table, layer, batch, seq)
        m, j = layer % 4, layer // 4
        if m == 0:
            qkv = _qkv_rope(xt, norm_mix[layer], mod, moba_wqkv[j], seq)
            o = _moba_attention(qkv, batch, seq, d)
            xt = _proj_residual(o, moba_wo, j, xt, mod, 2)
        elif m == 1:
            h = _prenorm(xt, norm_mix[layer], mod, 0)
            v = _short_conv_in(h, sc_w_in[j], sc_conv[j], seq)
            xt = _proj_residual(v, sc_w_out, j, xt, mod, 2)
        elif m == 2:
            xt = _pool_mixer(xt, norm_mix[layer], mod, pool_w[j], pool_scale[j], seq)
        else:
            u = _conformer_glu(xt, norm_mix[layer], mod, cf_w1[j], cf_b1[j])
            z = _conformer_conv(u, cf_dw[j], cf_dw_b[j], cf_ln_g[j], cf_ln_b[j], seq)
            xt = _proj_residual(z, cf_w2, j, xt, mod, 2, bias=cf_b2[j])

        i = layer // 2
        if layer % 2 == 0:
            xt = _dense_ffn(xt, norm_ffn[layer], ffn_wg, ffn_wu, ffn_wd, i, mod)
        else:
            last = layer == depth - 1
            xt = _moe_ffn(xt, norm_ffn[layer], mod, moe_router[i], moe_wg, moe_wu, moe_wd, i,
                          final_gain=norm_final if last else None)
    if depth % 2 == 1:
        xt = _final_norm(xt, norm_final)
    return xt.reshape(batch, seq, d)
```

```python
import functools

import jax
import jax.numpy as jnp
from jax import lax
from jax.experimental import pallas as pl
from jax.experimental.pallas import tpu as pltpu

F32 = jnp.float32
BF16 = jnp.bfloat16
I32 = jnp.int32
U32 = jnp.uint32

HEAD_DIM = 128
MOBA_BLOCK = 256
MOBA_TOPK = 3
ROPE_THETA = 10000.0
POOL_WINDOWS = (2, 4, 8, 16)
MOE_TOPK = 2
NORM_EPS = 1e-6
LN_EPS = 1e-5
NEG_INF = -1e30

LANES = 128
SUBLANES = 8
MXU_COLS = 256
VMEM_LIMIT_CAP = 60000 * 1024
MOD_PARTS = 6
STAGE_SLAB_BYTES = 12 << 20
CAST_ROWS = 256
ROW_CHUNK = 256


def _params(vmem_bytes, n_axes):
    limit = min(int(vmem_bytes) + (6 << 20), VMEM_LIMIT_CAP)
    return pltpu.CompilerParams(
        dimension_semantics=("arbitrary",) * n_axes, vmem_limit_bytes=limit)


def _dot(a, b):
    return jnp.dot(a, b, preferred_element_type=F32)


def _dot_nt(a, b):
    return lax.dot_general(a, b, (((1,), (1,)), ((), ())), preferred_element_type=F32)


def _silu(x):
    return x * jax.nn.sigmoid(x)


def _adaln(x, g, sc, sh):
    ms = jnp.mean(x * x, axis=-1, keepdims=True)
    return (x * lax.rsqrt(ms + NORM_EPS)) * g * (1.0 + sc) + sh


def _mod_kernel(c_ref, w_ref, b_ref, o_ref):
    c = c_ref[...]
    ca = _silu(c).astype(BF16)
    o_ref[...] = _dot(ca, w_ref[...].astype(BF16)) + b_ref[...]


def _modulation(c, mod_w, mod_b):
    depth, d, n = mod_w.shape
    b = c.shape[0]
    tn = 1024
    out = pl.pallas_call(
        _mod_kernel,
        grid=(depth, n // tn),
        in_specs=[
            pl.BlockSpec((b, d), lambda l, j: (0, 0)),
            pl.BlockSpec((None, d, tn), lambda l, j: (l, 0, j)),
            pl.BlockSpec((None, 1, tn), lambda l, j: (l, 0, j)),
        ],
        out_specs=pl.BlockSpec((None, b, tn), lambda l, j: (l, 0, j)),
        out_shape=jax.ShapeDtypeStruct((depth, b, n), F32),
        compiler_params=_params(2 * d * tn * 4 + d * tn * 2, 2),
        name="modulation",
    )(c, mod_w, mod_b.reshape(depth, 1, n))
    return out.reshape(depth * b * MOD_PARTS, 1, d)


class _Mod:
    def __init__(self, table, layer, batch, seq):
        self.table = table
        self.base = layer * batch * MOD_PARTS
        self.seq = seq

    def row(self, part, tok0):
        return self.base + (tok0 // self.seq) * MOD_PARTS + part


def _prenorm_kernel(x_ref, g_ref, sc_ref, sh_ref, o_ref):
    o_ref[...] = _adaln(x_ref[...], g_ref[...], sc_ref[...], sh_ref[...]).astype(o_ref.dtype)


def _prenorm(x, gain, mod, part0, tm=512):
    t, d = x.shape
    row = lambda p: pl.BlockSpec((None, 1, d), lambda i: (mod.row(p, i * tm), 0, 0))
    return pl.pallas_call(
        _prenorm_kernel,
        grid=(t // tm,),
        in_specs=[
            pl.BlockSpec((tm, d), lambda i: (i, 0)),
            pl.BlockSpec((1, d), lambda i: (0, 0)),
            row(part0 + 1), row(part0),
        ],
        out_specs=pl.BlockSpec((tm, d), lambda i: (i, 0)),
        out_shape=jax.ShapeDtypeStruct((t, d), BF16),
        compiler_params=_params(2 * tm * d * 6 + 3 * tm * d * 4, 1),
        name="prenorm",
    )(x, gain.reshape(1, d), mod.table, mod.table)


def _final_norm_kernel(x_ref, g_ref, o_ref):
    x = x_ref[...]
    ms = jnp.mean(x * x, axis=-1, keepdims=True)
    o_ref[...] = x * lax.rsqrt(ms + NORM_EPS) * g_ref[...]


def _final_norm(x, gain, tm=512):
    t, d = x.shape
    return pl.pallas_call(
        _final_norm_kernel,
        grid=(t // tm,),
        in_specs=[pl.BlockSpec((tm, d), lambda i: (i, 0)), pl.BlockSpec((1, d), lambda i: (0, 0))],
        out_specs=pl.BlockSpec((tm, d), lambda i: (i, 0)),
        out_shape=jax.ShapeDtypeStruct((t, d), F32),
        compiler_params=_params(6 * tm * d * 4, 1),
        name="final_norm",
    )(x, gain.reshape(1, d))


def _qkv_kernel(x_ref, gn_ref, sc_ref, sh_ref, w_ref, cos_ref, sin_ref, o_ref, wb_ref, *, nq, scale):
    j = pl.program_id(0)
    tm, tn = o_ref.shape

    @pl.when(pl.program_id(1) == 0)
    def _():
        wb_ref[...] = w_ref[...].astype(BF16)

    def tile(mult):
        for r0 in range(0, tm, ROW_CHUNK):
            rs = slice(r0, r0 + ROW_CHUNK)
            h = _adaln(x_ref[rs, :], gn_ref[...], sc_ref[...], sh_ref[...]).astype(BF16)
            acc = _dot(h, wb_ref[...])
            if mult is None:
                o_ref[rs, :] = acc.astype(o_ref.dtype)
                continue
            cos = cos_ref[rs, :]
            sin = sin_ref[rs, :]
            for c in range(tn // HEAD_DIM):
                a = acc[:, c * HEAD_DIM:(c + 1) * HEAD_DIM]
                r = a * cos + pltpu.roll(a, HEAD_DIM // 2, 1) * sin
                if mult != 1.0:
                    r = r * mult
                o_ref[rs, c * HEAD_DIM:(c + 1) * HEAD_DIM] = r.astype(o_ref.dtype)

    @pl.when(j < nq)
    def _():
        tile(scale)

    @pl.when((j >= nq) & (j < 2 * nq))
    def _():
        tile(1.0)

    @pl.when(j >= 2 * nq)
    def _():
        tile(None)


def _qkv_rope(x, gain, mod, w, seq, tm=1024, tn=1024):
    t, d = x.shape
    n = w.shape[1]
    tn = min(tn, d)
    half = HEAD_DIM // 2
    inv = ROPE_THETA ** (-jnp.arange(half, dtype=F32) * (2.0 / HEAD_DIM))
    ang = jnp.arange(seq, dtype=F32)[:, None] * inv[None, :]
    cos = jnp.concatenate([jnp.cos(ang), jnp.cos(ang)], axis=-1)
    sin = jnp.concatenate([-jnp.sin(ang), jnp.sin(ang)], axis=-1)
    tm = min(tm, seq)
    sblk = seq // tm
    kern = functools.partial(_qkv_kernel, nq=d // tn, scale=HEAD_DIM ** -0.5)
    row = lambda p: pl.BlockSpec((None, 1, d), lambda j, i: (mod.row(p, i * tm), 0, 0))
    return pl.pallas_call(
        kern,
        grid=(n // tn, t // tm),
        in_specs=[
            pl.BlockSpec((tm, d), lambda j, i: (i, 0)),
            pl.BlockSpec((1, d), lambda j, i: (0, 0)),
            row(1), row(0),
            pl.BlockSpec((d, tn), lambda j, i: (0, j)),
            pl.BlockSpec((tm, HEAD_DIM), lambda j, i: (i % sblk, 0)),
            pl.BlockSpec((tm, HEAD_DIM), lambda j, i: (i % sblk, 0)),
        ],
        out_specs=pl.BlockSpec((tm, tn), lambda j, i: (i, j)),
        out_shape=jax.ShapeDtypeStruct((t, n), BF16),
        scratch_shapes=[pltpu.VMEM((d, tn), BF16)],
        compiler_params=_params(
            2 * tm * d * 4 + 2 * d * tn * 4 + d * tn * 2 + 2 * tm * tn * 2
            + 4 * ROW_CHUNK * (d + tn) * 4, 2),
        name="qkv_rope",
    )(x, gain.reshape(1, d), mod.table, mod.table, w, cos, sin)


def _moba_kernel(q_ref, k_ref, v_ref, o_ref, kbar_ref, vt_ref, s_ref, p_ref, *, nb, blk, topk):
    kr = kbar_ref.shape[0]

    kbar_ref[...] = jnp.zeros(kbar_ref.shape, F32)
    for m in range(nb):
        rows = slice(m * blk, (m + 1) * blk)
        kbar_ref[m:m + 1, :] = jnp.mean(k_ref[rows, :].astype(F32), axis=0, keepdims=True)
        vt_ref[:, rows] = v_ref[rows, :].astype(F32).T.astype(BF16)
    kbar = kbar_ref[...]
    kb_hi = kbar.astype(BF16)
    kb_lo = (kbar - kb_hi.astype(F32)).astype(BF16)

    blk_id = lax.broadcasted_iota(I32, (kr, blk), 0)
    causal = lax.broadcasted_iota(I32, (blk, blk), 0) <= lax.broadcasted_iota(I32, (blk, blk), 1)

    for n in range(nb):
        q = q_ref[n * blk:(n + 1) * blk, :]
        if n > 0:
            gate = _dot_nt(kb_hi, q) + _dot_nt(kb_lo, q)
            valid = blk_id < n
            g = jnp.where(valid, gate, NEG_INF)
            cnt = jnp.zeros((kr, blk), F32)
            for mp in range(n):
                other = g[mp:mp + 1, :]
                cnt = cnt + jnp.where(other > g, 1.0, jnp.where((other == g) & (blk_id > mp), 1.0, 0.0))
            sel = jnp.where(valid & (cnt < topk), 1.0, 0.0)

        mx = None
        for m in range(n + 1):
            rows = slice(m * blk, (m + 1) * blk)
            s = _dot_nt(k_ref[rows, :], q)
            s = jnp.where(causal if m == n else sel[m:m + 1, :] > 0.0, s, NEG_INF)
            s_ref[rows, :] = s
            bm = jnp.max(s, axis=0, keepdims=True)
            mx = bm if mx is None else jnp.maximum(mx, bm)
        denom = jnp.zeros((1, blk), F32)
        for m in range(n + 1):
            rows = slice(m * blk, (m + 1) * blk)
            p = jnp.exp(s_ref[rows, :] - mx)
            denom = denom + jnp.sum(p, axis=0, keepdims=True)
            p_ref[rows, :] = p.astype(BF16)
        keys = (n + 1) * blk
        acc = _dot(vt_ref[:, 0:keys], p_ref[0:keys, :])
        o_ref[n * blk:(n + 1) * blk, :] = (acc / denom).T.astype(o_ref.dtype)


def _moba_attention(qkv, batch, seq, d):
    t = qkv.shape[0]
    heads = d // HEAD_DIM
    blk = MOBA_BLOCK
    assert seq % blk == 0
    nb = seq // blk
    kr = 16
    assert nb <= kr
    kern = functools.partial(_moba_kernel, nb=nb, blk=blk, topk=min(MOBA_TOPK, nb - 1))
    return pl.pallas_call(
        kern,
        grid=(batch, heads),
        in_specs=[
            pl.BlockSpec((seq, HEAD_DIM), lambda b, h: (b, h)),
            pl.BlockSpec((seq, HEAD_DIM), lambda b, h: (b, heads + h)),
            pl.BlockSpec((seq, HEAD_DIM), lambda b, h: (b, 2 * heads + h)),
        ],
        out_specs=pl.BlockSpec((seq, HEAD_DIM), lambda b, h: (b, h)),
        out_shape=jax.ShapeDtypeStruct((t, d), BF16),
        scratch_shapes=[pltpu.VMEM((kr, HEAD_DIM), F32), pltpu.VMEM((HEAD_DIM, seq), BF16),
                        pltpu.VMEM((seq, blk), F32), pltpu.VMEM((seq, blk), BF16)],
        compiler_params=_params(10 * seq * HEAD_DIM * 2 + seq * blk * 6 + 16 * blk * blk * 4, 2),
        name="moba_attention",
    )(qkv, qkv, qkv)


def _proj_res_kernel(*refs, has_bias):
    if has_bias:
        a_ref, w_ref, r_ref, g_ref, b_ref, o_ref, wb_ref = refs
    else:
        a_ref, w_ref, r_ref, g_ref, o_ref, wb_ref = refs

    @pl.when(pl.program_id(1) == 0)
    def _():
        wb_ref[...] = w_ref[...].astype(BF16)

    y = _dot(a_ref[...], wb_ref[...])
    if has_bias:
        y = y + b_ref[...]
    o_ref[...] = r_ref[...] + g_ref[...] * y


def _proj_residual(a, w3, widx, resid, mod, gate_part, bias=None, tm=1024, tn=512):
    t, k = a.shape
    n = w3.shape[2]
    in_specs = [
        pl.BlockSpec((tm, k), lambda j, i: (i, 0)),
        pl.BlockSpec((None, k, tn), lambda j, i: (widx, 0, j)),
        pl.BlockSpec((tm, tn), lambda j, i: (i, j)),
        pl.BlockSpec((None, 1, tn), lambda j, i: (mod.row(gate_part, i * tm), 0, j)),
    ]
    args = [a, w3, resid, mod.table]
    if bias is not None:
        in_specs.append(pl.BlockSpec((1, tn), lambda j, i: (0, j)))
        args.append(bias.reshape(1, n))
    return pl.pallas_call(
        functools.partial(_proj_res_kernel, has_bias=bias is not None),
        grid=(n // tn, t // tm),
        in_specs=in_specs,
        out_specs=pl.BlockSpec((tm, tn), lambda j, i: (i, j)),
        out_shape=jax.ShapeDtypeStruct((t, n), F32),
        scratch_shapes=[pltpu.VMEM((k, tn), BF16)],
        compiler_params=_params(
            2 * tm * k * 2 + 2 * k * tn * 4 + k * tn * 2 + 5 * tm * tn * 4, 2),
        name="proj_residual",
    )(*args)


def _shift_rows(p, halo, k):
    hr = halo.shape[0]
    body = pltpu.roll(p, k, 0)
    head = jnp.where(lax.broadcasted_iota(I32, (hr, p.shape[1]), 0) < k,
                     pltpu.roll(halo, k, 0), body[:hr])
    return head, body


def _shortconv_kernel(x_ref, wb_g, wc_g, wx_g, cw_ref, o_ref, wb_ref, halo_ref, *, width, tiles_per_seq):
    i = pl.program_id(1)

    @pl.when(i == 0)
    def _():
        wb_ref[0] = wb_g[...].astype(BF16)
        wb_ref[1] = wc_g[...].astype(BF16)
        wb_ref[2] = wx_g[...].astype(BF16)

    @pl.when(i % tiles_per_seq == 0)
    def _():
        halo_ref[...] = jnp.zeros(halo_ref.shape, F32)

    x = x_ref[...]
    gb = _dot(x, wb_ref[0])
    p = _dot(x, wb_ref[1]) * _dot(x, wb_ref[2])
    hr = halo_ref.shape[0]
    halo = halo_ref[...]
    cw = cw_ref[...]
    body = cw[width - 1:width, :] * p
    head = body[:hr]
    for s in range(1, width):
        hs, bs = _shift_rows(p, halo, s)
        w = cw[width - 1 - s:width - s, :]
        body = body + w * bs
        head = head + w * hs
    o_ref[...] = (gb * body).astype(o_ref.dtype)
    o_ref[0:hr, :] = (gb[:hr] * head).astype(o_ref.dtype)
    halo_ref[...] = p[p.shape[0] - hr:, :]


def _short_conv_in(h, w_in, conv_w, seq, tm=1024, tn=256):
    t, d = h.shape
    width = conv_w.shape[0]
    nd = d // tn
    hr = 16
    assert width - 1 <= hr
    kern = functools.partial(_shortconv_kernel, width=width, tiles_per_seq=seq // tm)
    wspec = lambda off: pl.BlockSpec((d, tn), lambda j, i: (0, off * nd + j))
    return pl.pallas_call(
        kern,
        grid=(nd, t // tm),
        in_specs=[
            pl.BlockSpec((tm, d), lambda j, i: (i, 0)),
            wspec(0), wspec(1), wspec(2),
            pl.BlockSpec((width, tn), lambda j, i: (0, j)),
        ],
        out_specs=pl.BlockSpec((tm, tn), lambda j, i: (i, j)),
        out_shape=jax.ShapeDtypeStruct((t, d), BF16),
        scratch_shapes=[pltpu.VMEM((3, d, tn), BF16), pltpu.VMEM((hr, tn), F32)],
        compiler_params=_params(
            2 * tm * d * 2 + 6 * d * tn * 4 + 3 * d * tn * 2 + 10 * tm * tn * 4, 2),
        name="short_conv_in",
    )(h, w_in, w_in, w_in, conv_w)


def _pool_kernel(x_ref, xh_ref, gn_ref, sc_ref, sh_ref, w_ref, ps_ref, ga_ref, o_ref,
                 wb_ref, ext_ref, *, windows, tiles_per_seq):
    i = pl.program_id(0)
    tm, d = x_ref.shape
    hr = xh_ref.shape[0]
    cg = d // len(windows)

    @pl.when(i == 0)
    def _():
        wb_ref[...] = w_ref[...].astype(BF16)

    x = x_ref[...]
    gn, sc, sh = gn_ref[...], sc_ref[...], sh_ref[...]
    h = _adaln(x, gn, sc, sh)
    hh = _adaln(xh_ref[...], gn, sc, sh)
    first = i % tiles_per_seq == 0
    ext_ref[0:hr, :] = jnp.where(first, 0.0, hh)
    ext_ref[hr:, :] = h
    pos = (i % tiles_per_seq) * tm + lax.broadcasted_iota(I32, (tm, 1), 0)
    for g, w in enumerate(windows):
        cols = slice(g * cg, (g + 1) * cg)
        s = ext_ref[:, cols]
        step = 1
        while step < w:
            s = s + pltpu.roll(s, step, 0)
            step *= 2
        count = jnp.minimum(pos + 1, w).astype(F32)
        pooled = s[hr:] / count - h[:, cols]
        mixed = _dot(pooled.astype(BF16), wb_ref[g])
        o_ref[:, cols] = x[:, cols] + ga_ref[:, cols] * (mixed * ps_ref[:, cols])


def _pool_mixer(x, gain, mod, w_group, scale, seq, tm=256):
    t, d = x.shape
    g, cg, _ = w_group.shape
    assert g == len(POOL_WINDOWS) and cg % LANES == 0
    hr = 16
    assert max(POOL_WINDOWS) <= hr and all(w & (w - 1) == 0 for w in POOL_WINDOWS)
    kern = functools.partial(_pool_kernel, windows=POOL_WINDOWS, tiles_per_seq=seq // tm)
    row = lambda p: pl.BlockSpec((None, 1, d), lambda i: (mod.row(p, i * tm), 0, 0))
    return pl.pallas_call(
        kern,
        grid=(t // tm,),
        in_specs=[
            pl.BlockSpec((tm, d), lambda i: (i, 0)),
            pl.BlockSpec((hr, d), lambda i: (jnp.maximum(i * (tm // hr) - 1, 0), 0)),
            pl.BlockSpec((1, d), lambda i: (0, 0)),
            row(1), row(0),
            pl.BlockSpec((g, cg, cg), lambda i: (0, 0, 0)),
            pl.BlockSpec((1, d), lambda i: (0, 0)),
            row(2),
        ],
        out_specs=pl.BlockSpec((tm, d), lambda i: (i, 0)),
        out_shape=jax.ShapeDtypeStruct((t, d), F32),
        scratch_shapes=[pltpu.VMEM((g, cg, cg), BF16), pltpu.VMEM((tm + hr, d), F32)],
        compiler_params=_params(4 * tm * d * 4 + 2 * g * cg * cg * 4 + g * cg * cg * 2
                                + 6 * tm * d * 4, 1),
        name="pool_mixer",
    )(x, x, gain.reshape(1, d), mod.table, mod.table, w_group, scale.reshape(1, d), mod.table)


def _glu_kernel(x_ref, gn_ref, sc_ref, sh_ref, wa_ref, wg_ref, ba_ref, bg_ref, o_ref, wb_ref):
    @pl.when(pl.program_id(1) == 0)
    def _():
        wb_ref[0] = wa_ref[...].astype(BF16)
        wb_ref[1] = wg_ref[...].astype(BF16)

    for r0 in range(0, x_ref.shape[0], ROW_CHUNK):
        rs = slice(r0, r0 + ROW_CHUNK)
        h = _adaln(x_ref[rs, :], gn_ref[...], sc_ref[...], sh_ref[...]).astype(BF16)
        a = _dot(h, wb_ref[0]) + ba_ref[...]
        g = _dot(h, wb_ref[1]) + bg_ref[...]
        o_ref[rs, :] = a * jax.nn.sigmoid(g)


def _conformer_glu(x, gain, mod, w1, b1, tm=1024, tn=512):
    t, d = x.shape
    n = w1.shape[1] // 2
    nd = n // tn
    b1 = b1.reshape(1, 2 * n)
    row = lambda p: pl.BlockSpec((None, 1, d), lambda j, i: (mod.row(p, i * tm), 0, 0))
    return pl.pallas_call(
        _glu_kernel,
        grid=(nd, t // tm),
        in_specs=[
            pl.BlockSpec((tm, d), lambda j, i: (i, 0)),
            pl.BlockSpec((1, d), lambda j, i: (0, 0)),
            row(1), row(0),
            pl.BlockSpec((d, tn), lambda j, i: (0, j)),
            pl.BlockSpec((d, tn), lambda j, i: (0, nd + j)),
            pl.BlockSpec((1, tn), lambda j, i: (0, j)),
            pl.BlockSpec((1, tn), lambda j, i: (0, nd + j)),
        ],
        out_specs=pl.BlockSpec((tm, tn), lambda j, i: (i, j)),
        out_shape=jax.ShapeDtypeStruct((t, n), F32),
        scratch_shapes=[pltpu.VMEM((2, d, tn), BF16)],
        compiler_params=_params(
            2 * tm * d * 4 + 4 * d * tn * 4 + 2 * d * tn * 2 + 2 * tm * tn * 4
            + 4 * ROW_CHUNK * (d + tn) * 4, 2),
        name="conformer_glu",
    )(x, gain.reshape(1, d), mod.table, mod.table, w1, w1, b1, b1)


def _cfconv_kernel(u_ref, uh_ref, dw_ref, dwb_ref, lg_ref, lb_ref, o_ref, ph_ref, y_ref,
                   *, width, tiles_per_seq, rc, cc):
    i = pl.program_id(0)
    tm, d = u_ref.shape
    hr = uh_ref.shape[0]
    rows = tm + hr
    first = i % tiles_per_seq == 0
    off = hr - (width - 1)
    for c0 in range(0, d, cc):
        cols = slice(c0, c0 + cc)
        ph_ref[0, 0:hr, :] = jnp.where(first, 0.0, uh_ref[:, cols])
        ph_ref[0, hr:, :] = u_ref[:, cols]
        e0 = ph_ref[0]
        for s in range(1, SUBLANES):
            ph_ref[s] = pltpu.roll(e0, rows - s, 0)
        w = dw_ref[:, cols]
        for r0 in range(0, tm, rc):
            acc = jnp.zeros((rc, cc), F32) + dwb_ref[:, cols]
            for k in range(width):
                q, s = divmod(off + k, SUBLANES)
                a0 = r0 + q * SUBLANES
                acc = acc + w[k:k + 1, :] * ph_ref[s, a0:a0 + rc, :]
            y_ref[r0:r0 + rc, cols] = acc
    y = y_ref[...]
    mu = jnp.mean(y, axis=-1, keepdims=True)
    yc = y - mu
    var = jnp.mean(yc * yc, axis=-1, keepdims=True)
    z = yc * lax.rsqrt(var + LN_EPS) * lg_ref[...] + lb_ref[...]
    o_ref[...] = _silu(z).astype(o_ref.dtype)


def _conformer_conv(u, dw, dw_b, ln_g, ln_b, seq, tm=256):
    t, d = u.shape
    width = dw.shape[0]
    hr = 32
    assert width - 1 <= hr
    cc = LANES
    kern = functools.partial(_cfconv_kernel, width=width, tiles_per_seq=seq // tm, rc=tm, cc=cc)
    vec = lambda: pl.BlockSpec((1, d), lambda i: (0, 0))
    return pl.pallas_call(
        kern,
        grid=(t // tm,),
        in_specs=[
            pl.BlockSpec((tm, d), lambda i: (i, 0)),
            pl.BlockSpec((hr, d), lambda i: (jnp.maximum(i * (tm // hr) - 1, 0), 0)),
            pl.BlockSpec((width, d), lambda i: (0, 0)),
            vec(), vec(), vec(),
        ],
        out_specs=pl.BlockSpec((tm, d), lambda i: (i, 0)),
        out_shape=jax.ShapeDtypeStruct((t, d), BF16),
        scratch_shapes=[pltpu.VMEM((SUBLANES, tm + hr, cc), F32), pltpu.VMEM((tm, d), F32)],
        compiler_params=_params(8 * tm * d * 4 + SUBLANES * (tm + hr) * cc * 4, 1),
        name="conformer_conv",
    )(u, u, dw, dw_b.reshape(1, d), ln_g.reshape(1, d), ln_b.reshape(1, d))


def _pick_cols(n, k, must_divide):
    best = None
    for c in range(MXU_COLS, n + 1, MXU_COLS):
        if k * c * 4 <= STAGE_SLAB_BYTES and (n % c == 0 or not must_divide):
            best = c
    assert best is not None
    return best


def _by_block(jj, n_blocks, full, last, fn):
    if last == full:
        fn(full)
    else:
        jj = jnp.asarray(jj, I32)
        pl.when(jj < n_blocks - 1)(lambda: fn(full))
        pl.when(jj == n_blocks - 1)(lambda: fn(last))


def _by_valid_rows(valid, total, compute, zero):
    @pl.when(valid > total - ROW_CHUNK)
    def _():
        compute(slice(0, total))

    @pl.when(valid <= total - ROW_CHUNK)
    def _():
        for r0 in range(0, total, ROW_CHUNK):
            rs = slice(r0, r0 + ROW_CHUNK)
            pl.when(r0 < valid)(functools.partial(compute, rs))
            pl.when(r0 >= valid)(functools.partial(zero, rs))


def _stage_weights(te_ref, nt_ref, nxt_ref, slabs, stage_ref, wb_ref, sem, last_cols):
    j, i = pl.program_id(0), pl.program_id(1)
    n_blocks = pl.num_programs(0)
    k, tn = stage_ref.shape[1], stage_ref.shape[2]

    def each_copy(e, jj, op):
        def run(width):
            for s, (w, col) in enumerate(slabs):
                op(pltpu.make_async_copy(
                    w.at[e, :, pl.ds(pl.multiple_of(col(jj) * tn, LANES), width)],
                    stage_ref.at[s, :, pl.ds(0, width)], sem.at[s]))

        _by_block(jj, n_blocks, tn, last_cols, run)

    start = lambda c: c.start()
    new_w = (i == 0) | (te_ref[i] != te_ref[jnp.maximum(i - 1, 0)])

    @pl.when(new_w)
    def _():
        @pl.when((i == 0) & (j == 0))
        def _():
            each_copy(te_ref[0], 0, start)

        each_copy(te_ref[i], j, lambda c: c.wait())

        def cast(c, carry):
            r = pl.multiple_of(c * CAST_ROWS, CAST_ROWS)
            for s in range(len(slabs)):
                wb_ref[s, pl.ds(r, CAST_ROWS), :] = stage_ref[s, pl.ds(r, CAST_ROWS), :].astype(BF16)
            return carry

        lax.fori_loop(0, k // CAST_ROWS, cast, 0)

        nxt = nxt_ref[i]
        more = nxt < nt_ref[0]

        @pl.when(more)
        def _():
            each_copy(te_ref[nxt], j, start)

        @pl.when(jnp.logical_not(more) & (j + 1 < n_blocks))
        def _():
            each_copy(te_ref[0], j + 1, start)


def _gateup_kernel(te_ref, nt_ref, nxt_ref, rv_ref, x_ref, gn_ref, sc_ref, sh_ref, wg_hbm, wu_hbm,
                   o_ref, stage_ref, wb_ref, sem, *, norm, last_cols):
    j, i = pl.program_id(0), pl.program_id(1)
    same = lambda jj: jj
    _stage_weights(te_ref, nt_ref, nxt_ref, [(wg_hbm, same), (wu_hbm, same)], stage_ref, wb_ref, sem,
                   last_cols)
    valid = rv_ref[i]

    def tile(width):
        def compute(rs):
            x = x_ref[rs, :]
            if norm:
                x = _adaln(x, gn_ref[...], sc_ref[...], sh_ref[...]).astype(BF16)
            g = _dot(x, wb_ref[0, :, :width])
            u = _dot(x, wb_ref[1, :, :width])
            o_ref[rs, :width] = (_silu(g) * u).astype(o_ref.dtype)

        def zero(rs):
            o_ref[rs, :width] = jnp.zeros((rs.stop - rs.start, width), o_ref.dtype)

        _by_valid_rows(valid, x_ref.shape[0], compute, zero)

    _by_block(j, pl.num_programs(0), o_ref.shape[1], last_cols, tile)


def _grouped_gateup(xs, wg, wu, tile_expert, n_tiles, next_group, rows_valid, tm, gain, mod, norm):
    p, d = xs.shape
    f = wg.shape[2]
    assert d % CAST_ROWS == 0 and tm % ROW_CHUNK == 0
    tf = _pick_cols(f, d, must_divide=False)
    n_blocks = pl.cdiv(f, tf)
    last_cols = f - (n_blocks - 1) * tf
    row = lambda part: pl.BlockSpec(
        (None, 1, d), lambda j, i, te, nt, nx, rv: (mod.row(part, i * tm if norm else 0), 0, 0))
    return pl.pallas_call(
        functools.partial(_gateup_kernel, norm=norm, last_cols=last_cols),
        grid_spec=pltpu.PrefetchScalarGridSpec(
            num_scalar_prefetch=4,
            grid=(n_blocks, p // tm),
            in_specs=[
                pl.BlockSpec((tm, d), lambda j, i, te, nt, nx, rv: (jnp.minimum(i, nt[0] - 1), 0)),
                pl.BlockSpec((1, d), lambda j, i, te, nt, nx, rv: (0, 0)),
                row(4), row(3),
                pl.BlockSpec(memory_space=pltpu.HBM),
                pl.BlockSpec(memory_space=pltpu.HBM),
            ],
            out_specs=pl.BlockSpec((tm, tf), lambda j, i, te, nt, nx, rv: (i, j)),
            scratch_shapes=[pltpu.VMEM((2, d, tf), F32), pltpu.VMEM((2, d, tf), BF16),
                            pltpu.SemaphoreType.DMA((2,))],
        ),
        out_shape=jax.ShapeDtypeStruct((p, f), BF16),
        compiler_params=_params(
            2 * d * tf * 6 + 2 * tm * d * xs.dtype.itemsize + 2 * tm * tf * 2
            + 4 * ROW_CHUNK * (tf + d) * 4, 2),
        name="grouped_gateup",
    )(tile_expert, n_tiles, next_group, rows_valid, xs, gain.reshape(1, d), mod.table, mod.table, wg, wu)


def _pack_pair(lo, hi):
    bl = lax.bitcast_convert_type(lo.astype(BF16).astype(F32), U32) >> 16
    bh = lax.bitcast_convert_type(hi.astype(BF16).astype(F32), U32) & jnp.uint32(0xFFFF0000)
    return bh | bl


def _unpack_pair(w):
    lo = lax.bitcast_convert_type(w << 16, F32)
    hi = lax.bitcast_convert_type(w & jnp.uint32(0xFFFF0000), F32)
    return lo, hi


def _down_kernel(te_ref, nt_ref, nxt_ref, rv_ref, a_ref, w_hbm, o_ref, stage_ref, wb_ref, sem, *, half_blocks):
    i = pl.program_id(1)
    slabs = [(w_hbm, lambda jj: jj), (w_hbm, lambda jj: half_blocks + jj)]
    _stage_weights(te_ref, nt_ref, nxt_ref, slabs, stage_ref, wb_ref, sem, stage_ref.shape[2])
    def compute(rs):
        a = a_ref[rs, :]
        o_ref[rs, :] = _pack_pair(_dot(a, wb_ref[0]), _dot(a, wb_ref[1]))

    def zero(rs):
        o_ref[rs, :] = jnp.zeros((rs.stop - rs.start, o_ref.shape[1]), o_ref.dtype)

    _by_valid_rows(rv_ref[i], a_ref.shape[0], compute, zero)


def _grouped_down(act, wd, tile_expert, n_tiles, next_group, rows_valid, tm):
    p, f = act.shape
    n = wd.shape[2]
    assert f % CAST_ROWS == 0 and tm % ROW_CHUNK == 0
    tn = _pick_cols(n // 2, f, must_divide=True)
    half_blocks = (n // 2) // tn
    return pl.pallas_call(
        functools.partial(_down_kernel, half_blocks=half_blocks),
        grid_spec=pltpu.PrefetchScalarGridSpec(
            num_scalar_prefetch=4,
            grid=(half_blocks, p // tm),
            in_specs=[
                pl.BlockSpec((tm, f), lambda j, i, te, nt, nx, rv: (jnp.minimum(i, nt[0] - 1), 0)),
                pl.BlockSpec(memory_space=pltpu.HBM),
            ],
            out_specs=pl.BlockSpec((tm, tn), lambda j, i, te, nt, nx, rv: (i, j)),
            scratch_shapes=[pltpu.VMEM((2, f, tn), F32), pltpu.VMEM((2, f, tn), BF16),
                            pltpu.SemaphoreType.DMA((2,))],
        ),
        out_shape=jax.ShapeDtypeStruct((p, n // 2), U32),
        compiler_params=_params(2 * f * tn * 6 + 2 * tm * f * 2 + 2 * tm * tn * 4 + 8 * ROW_CHUNK * tn * 4, 2),
        name="grouped_down",
    )(tile_expert, n_tiles, next_group, rows_valid, act, wd)


def _dense_ffn(x, gain, wg, wu, wd, widx, mod, tm=512):
    t = x.shape[0]
    nt = t // tm
    te = jnp.full((nt,), widx, I32)
    single_group = jnp.full((nt,), nt, I32)
    all_rows = jnp.full((nt,), tm, I32)
    act = _grouped_gateup(x, wg, wu, te, jnp.full((1,), nt, I32), single_group, all_rows, tm,
                          gain, mod, norm=True)
    return _proj_residual(act, wd, widx, x, mod, 5, tm=tm, tn=512)


def _router_kernel(x_ref, gn_ref, sc_ref, sh_ref, r_ref, h_ref, info_ref, cnt_ref, run_ref, *, n_exp):
    i = pl.program_id(0)
    tm = x_ref.shape[0]

    @pl.when(i == 0)
    def _():
        run_ref[...] = jnp.zeros(run_ref.shape, F32)

    h = _adaln(x_ref[...], gn_ref[...], sc_ref[...], sh_ref[...])
    half = h.shape[1] // 2
    h_ref[...] = _pack_pair(h[:, :half], h[:, half:])
    h_hi = h.astype(BF16)
    h_lo = (h - h_hi.astype(F32)).astype(BF16)
    r = r_ref[...]
    r_hi = r.astype(BF16)
    r_lo = (r - r_hi.astype(F32)).astype(BF16)
    logits = _dot(h_hi, r_hi) + (_dot(h_lo, r_hi) + _dot(h_hi, r_lo))

    lane = lax.broadcasted_iota(I32, (tm, LANES), 1).astype(F32)
    lg = jnp.where(lane < n_exp, logits, -jnp.inf)
    v1 = jnp.max(lg, axis=1, keepdims=True)
    i1 = jnp.min(jnp.where(lg == v1, lane, float(LANES)), axis=1, keepdims=True)
    lg2 = jnp.where(lane == i1, -jnp.inf, lg)
    v2 = jnp.max(lg2, axis=1, keepdims=True)
    i2 = jnp.min(jnp.where(lg2 == v2, lane, float(LANES)), axis=1, keepdims=True)
    e = jnp.exp(v2 - v1)
    g1 = 1.0 / (1.0 + e)
    g2 = e / (1.0 + e)

    oh = jnp.where((lane == i1) | (lane == i2), 1.0, 0.0)
    tri = jnp.where(lax.broadcasted_iota(I32, (tm, tm), 0) > lax.broadcasted_iota(I32, (tm, tm), 1),
                    1.0, 0.0).astype(BF16)
    cum = _dot(tri, oh.astype(BF16)) + run_ref[...]
    rank1 = jnp.sum(jnp.where(lane == i1, cum, 0.0), axis=1, keepdims=True)
    rank2 = jnp.sum(jnp.where(lane == i2, cum, 0.0), axis=1, keepdims=True)
    run = run_ref[...] + jnp.sum(oh, axis=0, keepdims=True)
    run_ref[...] = run
    cnt_ref[...] = jnp.broadcast_to(run, cnt_ref.shape)

    info = jnp.zeros((tm, LANES), F32)
    for k, val in enumerate((i1, i2, g1, g2, rank1, rank2)):
        info = jnp.where(lane == float(k), val, info)
    info_ref[...] = info


def _route(x, gain, mod, router, tm=256):
    t, d = x.shape
    n_exp = router.shape[1]
    assert n_exp <= LANES
    rpad = jnp.pad(router, ((0, 0), (0, LANES - n_exp)))
    row = lambda p: pl.BlockSpec((None, 1, d), lambda i: (mod.row(p, i * tm), 0, 0))
    return pl.pallas_call(
        functools.partial(_router_kernel, n_exp=n_exp),
        grid=(t // tm,),
        in_specs=[
            pl.BlockSpec((tm, d), lambda i: (i, 0)),
            pl.BlockSpec((1, d), lambda i: (0, 0)),
            row(4), row(3),
            pl.BlockSpec((d, LANES), lambda i: (0, 0)),
        ],
        out_specs=[
            pl.BlockSpec((tm, d // 2), lambda i: (i, 0)),
            pl.BlockSpec((tm, LANES), lambda i: (i, 0)),
            pl.BlockSpec((8, LANES), lambda i: (0, 0)),
        ],
        out_shape=[
            jax.ShapeDtypeStruct((t, d // 2), U32),
            jax.ShapeDtypeStruct((t, LANES), F32),
            jax.ShapeDtypeStruct((8, LANES), F32),
        ],
        scratch_shapes=[pltpu.VMEM((1, LANES), F32)],
        compiler_params=_params(4 * tm * d * 4 + 4 * tm * d * 4 + 2 * d * LANES * 4, 1),
        name="moe_route",
    )(x, gain.reshape(1, d), mod.table, mod.table, rpad)


ROW_DMA_UNROLL = 8


def _gather_kernel(tok_ref, tot_ref, hp_ref, o_ref, buf_ref):
    i = pl.program_id(0)
    tg, half = buf_ref.shape
    base = i * tg

    @pl.when(base < tot_ref[0])
    def _():
        def copy_row(r, c):
            buf_ref[pl.ds(r, 1), :] = hp_ref[pl.ds(tok_ref[base + r], 1), :]
            return c

        lax.fori_loop(0, tg, copy_row, 0, unroll=ROW_DMA_UNROLL)
        lo, hi = _unpack_pair(buf_ref[...])
        o_ref[:, :half] = lo.astype(o_ref.dtype)
        o_ref[:, half:] = hi.astype(o_ref.dtype)

    @pl.when(base >= tot_ref[0])
    def _():
        o_ref[...] = jnp.zeros(o_ref.shape, o_ref.dtype)


def _gather_rows(hp, sorted_tok, total_rows, tg=256):
    t, half = hp.shape
    p = sorted_tok.shape[0]
    return pl.pallas_call(
        _gather_kernel,
        grid_spec=pltpu.PrefetchScalarGridSpec(
            num_scalar_prefetch=2,
            grid=(p // tg,),
            in_specs=[pl.BlockSpec(memory_space=pltpu.VMEM)],
            out_specs=pl.BlockSpec((tg, 2 * half), lambda i, tok, tot: (i, 0)),
            scratch_shapes=[pltpu.VMEM((tg, half), U32)],
        ),
        out_shape=jax.ShapeDtypeStruct((p, 2 * half), BF16),
        compiler_params=_params(t * half * 4 + 8 * tg * half * 4, 1),
        name="moe_gather",
    )(sorted_tok, total_rows, hp)


def _combine_kernel(*refs, final):
    if final:
        pos_ref, y_hbm, x_ref, info_ref, gf_ref, nf_ref, o_ref, buf_ref, sem = refs
    else:
        pos_ref, y_hbm, x_ref, info_ref, gf_ref, o_ref, buf_ref, sem = refs
    i = pl.program_id(0)
    tc, half = buf_ref.shape[2], buf_ref.shape[3]

    def row_copy(slot, k, r, src_row):
        return pltpu.make_async_copy(
            y_hbm.at[pl.ds(src_row, 1)], buf_ref.at[slot, k, pl.ds(r, 1)], sem.at[slot])

    def fetch(tile):
        base = tile * tc

        def start(r, c):
            for k in range(MOE_TOPK):
                row_copy(tile % 2, k, r, pos_ref[MOE_TOPK * (base + r) + k]).start()
            return c

        lax.fori_loop(0, tc, start, 0, unroll=ROW_DMA_UNROLL)

    @pl.when(i == 0)
    def _():
        fetch(i)

    @pl.when(i + 1 < pl.num_programs(0))
    def _():
        fetch(i + 1)

    def wait(r, c):
        for k in range(MOE_TOPK):
            row_copy(i % 2, k, r, 0).wait()
        return c

    lax.fori_loop(0, tc, wait, 0, unroll=ROW_DMA_UNROLL)
    info = info_ref[...]
    y_lo = jnp.zeros((tc, half), F32)
    y_hi = jnp.zeros((tc, half), F32)
    for k in range(MOE_TOPK):
        lo, hi = _unpack_pair(buf_ref[i % 2, k])
        gate = info[:, MOE_TOPK + k:MOE_TOPK + k + 1]
        y_lo = y_lo + gate * lo
        y_hi = y_hi + gate * hi
    xn_lo = x_ref[:, :half] + gf_ref[:, :half] * y_lo
    xn_hi = x_ref[:, half:] + gf_ref[:, half:] * y_hi
    if final:
        ssq = jnp.sum(xn_lo * xn_lo, axis=-1, keepdims=True) + jnp.sum(xn_hi * xn_hi, axis=-1, keepdims=True)
        inv = lax.rsqrt(ssq / (2 * half) + NORM_EPS)
        xn_lo = xn_lo * inv * nf_ref[:, :half]
        xn_hi = xn_hi * inv * nf_ref[:, half:]
    o_ref[:, :half] = xn_lo
    o_ref[:, half:] = xn_hi


def _combine(y, pos_flat, x, info, mod, final_gain=None, tc=256):
    t, d = x.shape
    final = final_gain is not None
    in_specs = [
        pl.BlockSpec(memory_space=pl.ANY),
        pl.BlockSpec((tc, d), lambda i, pos: (i, 0)),
        pl.BlockSpec((tc, LANES), lambda i, pos: (i, 0)),
        pl.BlockSpec((None, 1, d), lambda i, pos: (mod.row(5, i * tc), 0, 0)),
    ]
    args = [pos_flat, y, x, info, mod.table]
    if final:
        in_specs.append(pl.BlockSpec((1, d), lambda i, pos: (0, 0)))
        args.append(final_gain.reshape(1, d))
    return pl.pallas_call(
        functools.partial(_combine_kernel, final=final),
        grid_spec=pltpu.PrefetchScalarGridSpec(
            num_scalar_prefetch=1,
            grid=(t // tc,),
            in_specs=in_specs,
            out_specs=pl.BlockSpec((tc, d), lambda i, pos: (i, 0)),
            scratch_shapes=[pltpu.VMEM((2, MOE_TOPK, tc, d // 2), U32),
                            pltpu.SemaphoreType.DMA((2,))],
        ),
        out_shape=jax.ShapeDtypeStruct((t, d), F32),
        compiler_params=_params(2 * tc * d * 4 + 4 * tc * d * 4 + 4 * tc * d * 4, 1),
        name="moe_combine",
    )(*args)


def _moe_ffn(x, gain, mod, router, wg, wu, wd, widx, final_gain=None, tm=512):
    t, d = x.shape
    n_exp = router.shape[1]
    wg, wu, wd = (w.reshape((-1,) + w.shape[2:]) for w in (wg, wu, wd))
    hf, info, cnt = _route(x, gain, mod, router)

    idx = info[:, 0:2].astype(I32)
    rank = info[:, 4:6].astype(I32)
    counts = cnt[0, :n_exp].astype(I32)
    padded = ((counts + tm - 1) // tm) * tm
    ends = jnp.cumsum(padded)
    starts = ends - padded
    pos = starts[idx] + rank
    p_max = MOE_TOPK * t + n_exp * tm
    tok = jnp.repeat(jnp.arange(t, dtype=I32), MOE_TOPK)
    sorted_tok = jnp.zeros((p_max,), I32).at[pos.reshape(-1)].set(
        tok, unique_indices=True, mode="promise_in_bounds")
    total = ends[-1:].astype(I32)
    n_tiles = total // tm
    tile_row = jnp.minimum(jnp.arange(p_max // tm, dtype=I32), n_tiles[0] - 1) * tm
    local_expert = jnp.minimum(
        jnp.sum((ends[None, :] <= tile_row[:, None]).astype(I32), axis=1), n_exp - 1)
    tile_expert = widx * n_exp + local_expert
    next_group = (ends // tm)[local_expert]
    tile_ids = jnp.arange(p_max // tm, dtype=I32)
    rows_valid = jnp.where(
        tile_ids < n_tiles[0],
        jnp.clip((starts + counts)[local_expert] - tile_ids * tm, 0, tm), 0).astype(I32)

    xs = _gather_rows(hf, sorted_tok, total)
    act = _grouped_gateup(xs, wg, wu, tile_expert, n_tiles, next_group, rows_valid, tm,
                          gain, mod, norm=False)
    y = _grouped_down(act, wd, tile_expert, n_tiles, next_group, rows_valid, tm)
    return _combine(y, pos.reshape(-1), x, info, mod, final_gain)


def kernel(x, c, mod_w, mod_b, norm_mix, norm_ffn, norm_final, moba_wqkv, moba_wo, sc_w_in, sc_conv,
           sc_w_out, pool_w, pool_scale, cf_w1, cf_b1, cf_dw, cf_dw_b, cf_ln_g, cf_ln_b, cf_w2, cf_b2,
           ffn_wg, ffn_wu, ffn_wd, moe_router, moe_wg, moe_wu, moe_wd):
    batch, seq, d = x.shape
    depth = mod_w.shape[0]
    table = _modulation(c, mod_w, mod_b)
    xt = x.reshape(batch * seq, d)
    for layer in range(depth):
        mod = _Mod(table, layer, batch, seq)
        m, j = layer % 4, layer // 4
        if m == 0:
            qkv = _qkv_rope(xt, norm_mix[layer], mod, moba_wqkv[j], seq)
            o = _moba_attention(qkv, batch, seq, d)
            xt = _proj_residual(o, moba_wo, j, xt, mod, 2)
        elif m == 1:
            h = _prenorm(xt, norm_mix[layer], mod, 0)
            v = _short_conv_in(h, sc_w_in[j], sc_conv[j], seq)
            xt = _proj_residual(v, sc_w_out, j, xt, mod, 2)
        elif m == 2:
            xt = _pool_mixer(xt, norm_mix[layer], mod, pool_w[j], pool_scale[j], seq)
        else:
            u = _conformer_glu(xt, norm_mix[layer], mod, cf_w1[j], cf_b1[j])
            z = _conformer_conv(u, cf_dw[j], cf_dw_b[j], cf_ln_g[j], cf_ln_b[j], seq)
            xt = _proj_residual(z, cf_w2, j, xt, mod, 2, bias=cf_b2[j])

        i = layer // 2
        if layer % 2 == 0:
            xt = _dense_ffn(xt, norm_ffn[layer], ffn_wg, ffn_wu, ffn_wd, i, mod)
        else:
            last = layer == depth - 1
            xt = _moe_ffn(xt, norm_ffn[layer], mod, moe_router[i], moe_wg, moe_wu, moe_wd, i,
                          final_gain=norm_final if last else None)
    if depth % 2 == 1:
        xt = _final_norm(xt, norm_final)
    return xt.reshape(batch, seq, d)
```

```python
import functools

import jax
import jax.numpy as jnp
from jax import lax
from jax.experimental import pallas as pl
from jax.experimental.pallas import tpu as pltpu

F32 = jnp.float32
BF16 = jnp.bfloat16
I32 = jnp.int32
U32 = jnp.uint32

HEAD_DIM = 128
MOBA_BLOCK = 256
MOBA_TOPK = 3
ROPE_THETA = 10000.0
POOL_WINDOWS = (2, 4, 8, 16)
MOE_TOPK = 2
NORM_EPS = 1e-6
LN_EPS = 1e-5
NEG_INF = -1e30

LANES = 128
SUBLANES = 8
MXU_COLS = 256
VMEM_LIMIT_CAP = 60000 * 1024
MOD_PARTS = 6
STAGE_SLAB_BYTES = 12 << 20
CAST_ROWS = 256
ROW_CHUNK = 256
PROLOGUE_CHUNK = 256


def _params(vmem_bytes, n_axes):
    limit = min(int(vmem_bytes) + (6 << 20), VMEM_LIMIT_CAP)
    return pltpu.CompilerParams(
        dimension_semantics=("arbitrary",) * n_axes, vmem_limit_bytes=limit)


def _dot(a, b):
    return jnp.dot(a, b, preferred_element_type=F32)


def _dot_nt(a, b):
    return lax.dot_general(a, b, (((1,), (1,)), ((), ())), preferred_element_type=F32)


def _silu(x):
    return x * jax.nn.sigmoid(x)


def _adaln(x, g, sc, sh):
    ms = jnp.mean(x * x, axis=-1, keepdims=True)
    return (x * lax.rsqrt(ms + NORM_EPS)) * g * (1.0 + sc) + sh


def _mod_kernel(c_ref, w_ref, b_ref, o_ref):
    c = c_ref[...]
    ca = _silu(c).astype(BF16)
    o_ref[...] = _dot(ca, w_ref[...].astype(BF16)) + b_ref[...]


def _modulation(c, mod_w, mod_b):
    depth, d, n = mod_w.shape
    b = c.shape[0]
    tn = 1024
    out = pl.pallas_call(
        _mod_kernel,
        grid=(depth, n // tn),
        in_specs=[
            pl.BlockSpec((b, d), lambda l, j: (0, 0)),
            pl.BlockSpec((None, d, tn), lambda l, j: (l, 0, j)),
            pl.BlockSpec((None, 1, tn), lambda l, j: (l, 0, j)),
        ],
        out_specs=pl.BlockSpec((None, b, tn), lambda l, j: (l, 0, j)),
        out_shape=jax.ShapeDtypeStruct((depth, b, n), F32),
        compiler_params=_params(2 * d * tn * 4 + d * tn * 2, 2),
        name="modulation",
    )(c, mod_w, mod_b.reshape(depth, 1, n))
    return out.reshape(depth * b * MOD_PARTS, 1, d)


class _Mod:
    def __init__(self, table, layer, batch, seq):
        self.table = table
        self.base = layer * batch * MOD_PARTS
        self.seq = seq

    def row(self, part, tok0):
        return self.base + (tok0 // self.seq) * MOD_PARTS + part


def _prenorm_kernel(x_ref, g_ref, sc_ref, sh_ref, o_ref):
    o_ref[...] = _adaln(x_ref[...], g_ref[...], sc_ref[...], sh_ref[...]).astype(o_ref.dtype)


def _prenorm(x, gain, mod, part0, tm=512):
    t, d = x.shape
    row = lambda p: pl.BlockSpec((None, 1, d), lambda i: (mod.row(p, i * tm), 0, 0))
    return pl.pallas_call(
        _prenorm_kernel,
        grid=(t // tm,),
        in_specs=[
            pl.BlockSpec((tm, d), lambda i: (i, 0)),
            pl.BlockSpec((1, d), lambda i: (0, 0)),
            row(part0 + 1), row(part0),
        ],
        out_specs=pl.BlockSpec((tm, d), lambda i: (i, 0)),
        out_shape=jax.ShapeDtypeStruct((t, d), BF16),
        compiler_params=_params(2 * tm * d * 6 + 3 * tm * d * 4, 1),
        name="prenorm",
    )(x, gain.reshape(1, d), mod.table, mod.table)


def _final_norm_kernel(x_ref, g_ref, o_ref):
    x = x_ref[...]
    ms = jnp.mean(x * x, axis=-1, keepdims=True)
    o_ref[...] = x * lax.rsqrt(ms + NORM_EPS) * g_ref[...]


def _final_norm(x, gain, tm=512):
    t, d = x.shape
    return pl.pallas_call(
        _final_norm_kernel,
        grid=(t // tm,),
        in_specs=[pl.BlockSpec((tm, d), lambda i: (i, 0)), pl.BlockSpec((1, d), lambda i: (0, 0))],
        out_specs=pl.BlockSpec((tm, d), lambda i: (i, 0)),
        out_shape=jax.ShapeDtypeStruct((t, d), F32),
        compiler_params=_params(6 * tm * d * 4, 1),
        name="final_norm",
    )(x, gain.reshape(1, d))


def _qkv_kernel(x_ref, gn_ref, sc_ref, sh_ref, w_ref, cos_ref, sin_ref, o_ref, wb_ref, *, nq, scale):
    j = pl.program_id(0)
    tm, tn = o_ref.shape

    @pl.when(pl.program_id(1) == 0)
    def _():
        wb_ref[...] = w_ref[...].astype(BF16)

    def tile(mult):
        for r0 in range(0, tm, PROLOGUE_CHUNK):
            rs = slice(r0, r0 + PROLOGUE_CHUNK)
            h = _adaln(x_ref[rs, :], gn_ref[...], sc_ref[...], sh_ref[...]).astype(BF16)
            acc = _dot(h, wb_ref[...])
            if mult is None:
                o_ref[rs, :] = acc.astype(o_ref.dtype)
                continue
            cos = cos_ref[rs, :]
            sin = sin_ref[rs, :]
            for c in range(tn // HEAD_DIM):
                a = acc[:, c * HEAD_DIM:(c + 1) * HEAD_DIM]
                r = a * cos + pltpu.roll(a, HEAD_DIM // 2, 1) * sin
                if mult != 1.0:
                    r = r * mult
                o_ref[rs, c * HEAD_DIM:(c + 1) * HEAD_DIM] = r.astype(o_ref.dtype)

    @pl.when(j < nq)
    def _():
        tile(scale)

    @pl.when((j >= nq) & (j < 2 * nq))
    def _():
        tile(1.0)

    @pl.when(j >= 2 * nq)
    def _():
        tile(None)


def _qkv_rope(x, gain, mod, w, seq, tm=1024, tn=1024):
    t, d = x.shape
    n = w.shape[1]
    tn = min(tn, d)
    half = HEAD_DIM // 2
    inv = ROPE_THETA ** (-jnp.arange(half, dtype=F32) * (2.0 / HEAD_DIM))
    ang = jnp.arange(seq, dtype=F32)[:, None] * inv[None, :]
    cos = jnp.concatenate([jnp.cos(ang), jnp.cos(ang)], axis=-1)
    sin = jnp.concatenate([-jnp.sin(ang), jnp.sin(ang)], axis=-1)
    tm = min(tm, seq)
    sblk = seq // tm
    kern = functools.partial(_qkv_kernel, nq=d // tn, scale=HEAD_DIM ** -0.5)
    row = lambda p: pl.BlockSpec((None, 1, d), lambda j, i: (mod.row(p, i * tm), 0, 0))
    return pl.pallas_call(
        kern,
        grid=(n // tn, t // tm),
        in_specs=[
            pl.BlockSpec((tm, d), lambda j, i: (i, 0)),
            pl.BlockSpec((1, d), lambda j, i: (0, 0)),
            row(1), row(0),
            pl.BlockSpec((d, tn), lambda j, i: (0, j)),
            pl.BlockSpec((tm, HEAD_DIM), lambda j, i: (i % sblk, 0)),
            pl.BlockSpec((tm, HEAD_DIM), lambda j, i: (i % sblk, 0)),
        ],
        out_specs=pl.BlockSpec((tm, tn), lambda j, i: (i, j)),
        out_shape=jax.ShapeDtypeStruct((t, n), BF16),
        scratch_shapes=[pltpu.VMEM((d, tn), BF16)],
        compiler_params=_params(
            2 * tm * d * 4 + 2 * d * tn * 4 + d * tn * 2 + 2 * tm * tn * 2
            + 4 * PROLOGUE_CHUNK * (d + tn) * 4, 2),
        name="qkv_rope",
    )(x, gain.reshape(1, d), mod.table, mod.table, w, cos, sin)


def _moba_kernel(q_ref, k_ref, v_ref, o_ref, kbar_ref, vt_ref, s_ref, p_ref, *, nb, blk, topk):
    kr = kbar_ref.shape[0]

    kbar_ref[...] = jnp.zeros(kbar_ref.shape, F32)
    for m in range(nb):
        rows = slice(m * blk, (m + 1) * blk)
        kbar_ref[m:m + 1, :] = jnp.mean(k_ref[rows, :].astype(F32), axis=0, keepdims=True)
        vt_ref[:, rows] = v_ref[rows, :].astype(F32).T.astype(BF16)
    kbar = kbar_ref[...]
    kb_hi = kbar.astype(BF16)
    kb_lo = (kbar - kb_hi.astype(F32)).astype(BF16)

    blk_id = lax.broadcasted_iota(I32, (kr, blk), 0)
    causal = lax.broadcasted_iota(I32, (blk, blk), 0) <= lax.broadcasted_iota(I32, (blk, blk), 1)

    for n in range(nb):
        q = q_ref[n * blk:(n + 1) * blk, :]
        if n > 0:
            gate = _dot_nt(kb_hi, q) + _dot_nt(kb_lo, q)
            valid = blk_id < n
            g = jnp.where(valid, gate, NEG_INF)
            cnt = jnp.zeros((kr, blk), F32)
            for mp in range(n):
                other = g[mp:mp + 1, :]
                cnt = cnt + jnp.where(other > g, 1.0, jnp.where((other == g) & (blk_id > mp), 1.0, 0.0))
            sel = jnp.where(valid & (cnt < topk), 1.0, 0.0)

        mx = None
        for m in range(n + 1):
            rows = slice(m * blk, (m + 1) * blk)
            s = _dot_nt(k_ref[rows, :], q)
            s = jnp.where(causal if m == n else sel[m:m + 1, :] > 0.0, s, NEG_INF)
            s_ref[rows, :] = s
            bm = jnp.max(s, axis=0, keepdims=True)
            mx = bm if mx is None else jnp.maximum(mx, bm)
        denom = jnp.zeros((1, blk), F32)
        for m in range(n + 1):
            rows = slice(m * blk, (m + 1) * blk)
            p = jnp.exp(s_ref[rows, :] - mx)
            denom = denom + jnp.sum(p, axis=0, keepdims=True)
            p_ref[rows, :] = p.astype(BF16)
        keys = (n + 1) * blk
        acc = _dot(vt_ref[:, 0:keys], p_ref[0:keys, :])
        o_ref[n * blk:(n + 1) * blk, :] = (acc / denom).T.astype(o_ref.dtype)


def _moba_attention(qkv, batch, seq, d):
    t = qkv.shape[0]
    heads = d // HEAD_DIM
    blk = MOBA_BLOCK
    assert seq % blk == 0
    nb = seq // blk
    kr = 16
    assert nb <= kr
    kern = functools.partial(_moba_kernel, nb=nb, blk=blk, topk=min(MOBA_TOPK, nb - 1))
    return pl.pallas_call(
        kern,
        grid=(batch, heads),
        in_specs=[
            pl.BlockSpec((seq, HEAD_DIM), lambda b, h: (b, h)),
            pl.BlockSpec((seq, HEAD_DIM), lambda b, h: (b, heads + h)),
            pl.BlockSpec((seq, HEAD_DIM), lambda b, h: (b, 2 * heads + h)),
        ],
        out_specs=pl.BlockSpec((seq, HEAD_DIM), lambda b, h: (b, h)),
        out_shape=jax.ShapeDtypeStruct((t, d), BF16),
        scratch_shapes=[pltpu.VMEM((kr, HEAD_DIM), F32), pltpu.VMEM((HEAD_DIM, seq), BF16),
                        pltpu.VMEM((seq, blk), F32), pltpu.VMEM((seq, blk), BF16)],
        compiler_params=_params(10 * seq * HEAD_DIM * 2 + seq * blk * 6 + 16 * blk * blk * 4, 2),
        name="moba_attention",
    )(qkv, qkv, qkv)


def _proj_res_kernel(*refs, has_bias):
    if has_bias:
        a_ref, w_ref, r_ref, g_ref, b_ref, o_ref, wb_ref = refs
    else:
        a_ref, w_ref, r_ref, g_ref, o_ref, wb_ref = refs

    @pl.when(pl.program_id(1) == 0)
    def _():
        wb_ref[...] = w_ref[...].astype(BF16)

    y = _dot(a_ref[...], wb_ref[...])
    if has_bias:
        y = y + b_ref[...]
    o_ref[...] = r_ref[...] + g_ref[...] * y


def _proj_residual(a, w3, widx, resid, mod, gate_part, bias=None, tm=1024, tn=512):
    t, k = a.shape
    n = w3.shape[2]
    in_specs = [
        pl.BlockSpec((tm, k), lambda j, i: (i, 0)),
        pl.BlockSpec((None, k, tn), lambda j, i: (widx, 0, j)),
        pl.BlockSpec((tm, tn), lambda j, i: (i, j)),
        pl.BlockSpec((None, 1, tn), lambda j, i: (mod.row(gate_part, i * tm), 0, j)),
    ]
    args = [a, w3, resid, mod.table]
    if bias is not None:
        in_specs.append(pl.BlockSpec((1, tn), lambda j, i: (0, j)))
        args.append(bias.reshape(1, n))
    return pl.pallas_call(
        functools.partial(_proj_res_kernel, has_bias=bias is not None),
        grid=(n // tn, t // tm),
        in_specs=in_specs,
        out_specs=pl.BlockSpec((tm, tn), lambda j, i: (i, j)),
        out_shape=jax.ShapeDtypeStruct((t, n), F32),
        scratch_shapes=[pltpu.VMEM((k, tn), BF16)],
        compiler_params=_params(
            2 * tm * k * 2 + 2 * k * tn * 4 + k * tn * 2 + 5 * tm * tn * 4, 2),
        name="proj_residual",
    )(*args)


def _shift_rows(p, halo, k):
    hr = halo.shape[0]
    body = pltpu.roll(p, k, 0)
    head = jnp.where(lax.broadcasted_iota(I32, (hr, p.shape[1]), 0) < k,
                     pltpu.roll(halo, k, 0), body[:hr])
    return head, body


def _shortconv_kernel(x_ref, wb_g, wc_g, wx_g, cw_ref, o_ref, wb_ref, halo_ref, *, width, tiles_per_seq):
    i = pl.program_id(1)

    @pl.when(i == 0)
    def _():
        wb_ref[0] = wb_g[...].astype(BF16)
        wb_ref[1] = wc_g[...].astype(BF16)
        wb_ref[2] = wx_g[...].astype(BF16)

    @pl.when(i % tiles_per_seq == 0)
    def _():
        halo_ref[...] = jnp.zeros(halo_ref.shape, F32)

    x = x_ref[...]
    gb = _dot(x, wb_ref[0])
    p = _dot(x, wb_ref[1]) * _dot(x, wb_ref[2])
    hr = halo_ref.shape[0]
    halo = halo_ref[...]
    cw = cw_ref[...]
    body = cw[width - 1:width, :] * p
    head = body[:hr]
    for s in range(1, width):
        hs, bs = _shift_rows(p, halo, s)
        w = cw[width - 1 - s:width - s, :]
        body = body + w * bs
        head = head + w * hs
    o_ref[...] = (gb * body).astype(o_ref.dtype)
    o_ref[0:hr, :] = (gb[:hr] * head).astype(o_ref.dtype)
    halo_ref[...] = p[p.shape[0] - hr:, :]


def _short_conv_in(h, w_in, conv_w, seq, tm=1024, tn=256):
    t, d = h.shape
    width = conv_w.shape[0]
    nd = d // tn
    hr = 16
    assert width - 1 <= hr
    kern = functools.partial(_shortconv_kernel, width=width, tiles_per_seq=seq // tm)
    wspec = lambda off: pl.BlockSpec((d, tn), lambda j, i: (0, off * nd + j))
    return pl.pallas_call(
        kern,
        grid=(nd, t // tm),
        in_specs=[
            pl.BlockSpec((tm, d), lambda j, i: (i, 0)),
            wspec(0), wspec(1), wspec(2),
            pl.BlockSpec((width, tn), lambda j, i: (0, j)),
        ],
        out_specs=pl.BlockSpec((tm, tn), lambda j, i: (i, j)),
        out_shape=jax.ShapeDtypeStruct((t, d), BF16),
        scratch_shapes=[pltpu.VMEM((3, d, tn), BF16), pltpu.VMEM((hr, tn), F32)],
        compiler_params=_params(
            2 * tm * d * 2 + 6 * d * tn * 4 + 3 * d * tn * 2 + 10 * tm * tn * 4, 2),
        name="short_conv_in",
    )(h, w_in, w_in, w_in, conv_w)


def _pool_kernel(x_ref, xh_ref, gn_ref, sc_ref, sh_ref, w_ref, ps_ref, ga_ref, o_ref,
                 wb_ref, ext_ref, *, windows, tiles_per_seq):
    i = pl.program_id(0)
    tm, d = x_ref.shape
    hr = xh_ref.shape[0]
    cg = d // len(windows)

    @pl.when(i == 0)
    def _():
        wb_ref[...] = w_ref[...].astype(BF16)

    x = x_ref[...]
    gn, sc, sh = gn_ref[...], sc_ref[...], sh_ref[...]
    h = _adaln(x, gn, sc, sh)
    hh = _adaln(xh_ref[...], gn, sc, sh)
    first = i % tiles_per_seq == 0
    ext_ref[0:hr, :] = jnp.where(first, 0.0, hh)
    ext_ref[hr:, :] = h
    pos = (i % tiles_per_seq) * tm + lax.broadcasted_iota(I32, (tm, 1), 0)
    for g, w in enumerate(windows):
        cols = slice(g * cg, (g + 1) * cg)
        s = ext_ref[:, cols]
        step = 1
        while step < w:
            s = s + pltpu.roll(s, step, 0)
            step *= 2
        count = jnp.minimum(pos + 1, w).astype(F32)
        pooled = s[hr:] / count - h[:, cols]
        mixed = _dot(pooled.astype(BF16), wb_ref[g])
        o_ref[:, cols] = x[:, cols] + ga_ref[:, cols] * (mixed * ps_ref[:, cols])


def _pool_mixer(x, gain, mod, w_group, scale, seq, tm=512):
    t, d = x.shape
    g, cg, _ = w_group.shape
    assert g == len(POOL_WINDOWS) and cg % LANES == 0
    hr = 16
    assert max(POOL_WINDOWS) <= hr and all(w & (w - 1) == 0 for w in POOL_WINDOWS)
    kern = functools.partial(_pool_kernel, windows=POOL_WINDOWS, tiles_per_seq=seq // tm)
    row = lambda p: pl.BlockSpec((None, 1, d), lambda i: (mod.row(p, i * tm), 0, 0))
    return pl.pallas_call(
        kern,
        grid=(t // tm,),
        in_specs=[
            pl.BlockSpec((tm, d), lambda i: (i, 0)),
            pl.BlockSpec((hr, d), lambda i: (jnp.maximum(i * (tm // hr) - 1, 0), 0)),
            pl.BlockSpec((1, d), lambda i: (0, 0)),
            row(1), row(0),
            pl.BlockSpec((g, cg, cg), lambda i: (0, 0, 0)),
            pl.BlockSpec((1, d), lambda i: (0, 0)),
            row(2),
        ],
        out_specs=pl.BlockSpec((tm, d), lambda i: (i, 0)),
        out_shape=jax.ShapeDtypeStruct((t, d), F32),
        scratch_shapes=[pltpu.VMEM((g, cg, cg), BF16), pltpu.VMEM((tm + hr, d), F32)],
        compiler_params=_params(4 * tm * d * 4 + 2 * g * cg * cg * 4 + g * cg * cg * 2
                                + 6 * tm * d * 4, 1),
        name="pool_mixer",
    )(x, x, gain.reshape(1, d), mod.table, mod.table, w_group, scale.reshape(1, d), mod.table)


def _glu_kernel(x_ref, gn_ref, sc_ref, sh_ref, wa_ref, wg_ref, ba_ref, bg_ref, o_ref, wb_ref):
    @pl.when(pl.program_id(1) == 0)
    def _():
        wb_ref[0] = wa_ref[...].astype(BF16)
        wb_ref[1] = wg_ref[...].astype(BF16)

    for r0 in range(0, x_ref.shape[0], PROLOGUE_CHUNK):
        rs = slice(r0, r0 + PROLOGUE_CHUNK)
        h = _adaln(x_ref[rs, :], gn_ref[...], sc_ref[...], sh_ref[...]).astype(BF16)
        a = _dot(h, wb_ref[0]) + ba_ref[...]
        g = _dot(h, wb_ref[1]) + bg_ref[...]
        o_ref[rs, :] = a * jax.nn.sigmoid(g)


def _conformer_glu(x, gain, mod, w1, b1, tm=1024, tn=512):
    t, d = x.shape
    n = w1.shape[1] // 2
    nd = n // tn
    b1 = b1.reshape(1, 2 * n)
    row = lambda p: pl.BlockSpec((None, 1, d), lambda j, i: (mod.row(p, i * tm), 0, 0))
    return pl.pallas_call(
        _glu_kernel,
        grid=(nd, t // tm),
        in_specs=[
            pl.BlockSpec((tm, d), lambda j, i: (i, 0)),
            pl.BlockSpec((1, d), lambda j, i: (0, 0)),
            row(1), row(0),
            pl.BlockSpec((d, tn), lambda j, i: (0, j)),
            pl.BlockSpec((d, tn), lambda j, i: (0, nd + j)),
            pl.BlockSpec((1, tn), lambda j, i: (0, j)),
            pl.BlockSpec((1, tn), lambda j, i: (0, nd + j)),
        ],
        out_specs=pl.BlockSpec((tm, tn), lambda j, i: (i, j)),
        out_shape=jax.ShapeDtypeStruct((t, n), F32),
        scratch_shapes=[pltpu.VMEM((2, d, tn), BF16)],
        compiler_params=_params(
            2 * tm * d * 4 + 4 * d * tn * 4 + 2 * d * tn * 2 + 2 * tm * tn * 4
            + 4 * PROLOGUE_CHUNK * (d + tn) * 4, 2),
        name="conformer_glu",
    )(x, gain.reshape(1, d), mod.table, mod.table, w1, w1, b1, b1)


def _cfconv_kernel(u_ref, uh_ref, dw_ref, dwb_ref, lg_ref, lb_ref, o_ref, ph_ref, y_ref,
                   *, width, tiles_per_seq, rc, cc):
    i = pl.program_id(0)
    tm, d = u_ref.shape
    hr = uh_ref.shape[0]
    rows = tm + hr
    first = i % tiles_per_seq == 0
    off = hr - (width - 1)
    for c0 in range(0, d, cc):
        cols = slice(c0, c0 + cc)
        ph_ref[0, 0:hr, :] = jnp.where(first, 0.0, uh_ref[:, cols])
        ph_ref[0, hr:, :] = u_ref[:, cols]
        e0 = ph_ref[0]
        for s in range(1, SUBLANES):
            ph_ref[s] = pltpu.roll(e0, rows - s, 0)
        w = dw_ref[:, cols]
        for r0 in range(0, tm, rc):
            acc = jnp.zeros((rc, cc), F32) + dwb_ref[:, cols]
            for k in range(width):
                q, s = divmod(off + k, SUBLANES)
                a0 = r0 + q * SUBLANES
                acc = acc + w[k:k + 1, :] * ph_ref[s, a0:a0 + rc, :]
            y_ref[r0:r0 + rc, cols] = acc
    y = y_ref[...]
    mu = jnp.mean(y, axis=-1, keepdims=True)
    yc = y - mu
    var = jnp.mean(yc * yc, axis=-1, keepdims=True)
    z = yc * lax.rsqrt(var + LN_EPS) * lg_ref[...] + lb_ref[...]
    o_ref[...] = _silu(z).astype(o_ref.dtype)


def _conformer_conv(u, dw, dw_b, ln_g, ln_b, seq, tm=256):
    t, d = u.shape
    width = dw.shape[0]
    hr = 32
    assert width - 1 <= hr
    cc = LANES
    kern = functools.partial(_cfconv_kernel, width=width, tiles_per_seq=seq // tm, rc=tm, cc=cc)
    vec = lambda: pl.BlockSpec((1, d), lambda i: (0, 0))
    return pl.pallas_call(
        kern,
        grid=(t // tm,),
        in_specs=[
            pl.BlockSpec((tm, d), lambda i: (i, 0)),
            pl.BlockSpec((hr, d), lambda i: (jnp.maximum(i * (tm // hr) - 1, 0), 0)),
            pl.BlockSpec((width, d), lambda i: (0, 0)),
            vec(), vec(), vec(),
        ],
        out_specs=pl.BlockSpec((tm, d), lambda i: (i, 0)),
        out_shape=jax.ShapeDtypeStruct((t, d), BF16),
        scratch_shapes=[pltpu.VMEM((SUBLANES, tm + hr, cc), F32), pltpu.VMEM((tm, d), F32)],
        compiler_params=_params(8 * tm * d * 4 + SUBLANES * (tm + hr) * cc * 4, 1),
        name="conformer_conv",
    )(u, u, dw, dw_b.reshape(1, d), ln_g.reshape(1, d), ln_b.reshape(1, d))


def _pick_cols(n, k, must_divide):
    best = None
    for c in range(MXU_COLS, n + 1, MXU_COLS):
        if k * c * 4 <= STAGE_SLAB_BYTES and (n % c == 0 or not must_divide):
            best = c
    assert best is not None
    return best


def _by_block(jj, n_blocks, full, last, fn):
    if last == full:
        fn(full)
    else:
        jj = jnp.asarray(jj, I32)
        pl.when(jj < n_blocks - 1)(lambda: fn(full))
        pl.when(jj == n_blocks - 1)(lambda: fn(last))


def _by_valid_rows(valid, total, compute, zero):
    @pl.when(valid > total - ROW_CHUNK)
    def _():
        compute(slice(0, total))

    @pl.when(valid <= total - ROW_CHUNK)
    def _():
        for r0 in range(0, total, ROW_CHUNK):
            rs = slice(r0, r0 + ROW_CHUNK)
            pl.when(r0 < valid)(functools.partial(compute, rs))
            pl.when(r0 >= valid)(functools.partial(zero, rs))


def _stage_weights(te_ref, nt_ref, nxt_ref, slabs, stage_ref, wb_ref, sem, last_cols):
    j, i = pl.program_id(0), pl.program_id(1)
    n_blocks = pl.num_programs(0)
    k, tn = stage_ref.shape[1], stage_ref.shape[2]

    def each_copy(e, jj, op):
        def run(width):
            for s, (w, col) in enumerate(slabs):
                op(pltpu.make_async_copy(
                    w.at[e, :, pl.ds(pl.multiple_of(col(jj) * tn, LANES), width)],
                    stage_ref.at[s, :, pl.ds(0, width)], sem.at[s]))

        _by_block(jj, n_blocks, tn, last_cols, run)

    start = lambda c: c.start()
    new_w = (i == 0) | (te_ref[i] != te_ref[jnp.maximum(i - 1, 0)])

    @pl.when(new_w)
    def _():
        @pl.when((i == 0) & (j == 0))
        def _():
            each_copy(te_ref[0], 0, start)

        each_copy(te_ref[i], j, lambda c: c.wait())

        def cast(c, carry):
            r = pl.multiple_of(c * CAST_ROWS, CAST_ROWS)
            for s in range(len(slabs)):
                wb_ref[s, pl.ds(r, CAST_ROWS), :] = stage_ref[s, pl.ds(r, CAST_ROWS), :].astype(BF16)
            return carry

        lax.fori_loop(0, k // CAST_ROWS, cast, 0)

        nxt = nxt_ref[i]
        more = nxt < nt_ref[0]

        @pl.when(more)
        def _():
            each_copy(te_ref[nxt], j, start)

        @pl.when(jnp.logical_not(more) & (j + 1 < n_blocks))
        def _():
            each_copy(te_ref[0], j + 1, start)


def _gateup_kernel(te_ref, nt_ref, nxt_ref, rv_ref, x_ref, gn_ref, sc_ref, sh_ref, wg_hbm, wu_hbm,
                   o_ref, stage_ref, wb_ref, sem, *, norm, last_cols):
    j, i = pl.program_id(0), pl.program_id(1)
    same = lambda jj: jj
    _stage_weights(te_ref, nt_ref, nxt_ref, [(wg_hbm, same), (wu_hbm, same)], stage_ref, wb_ref, sem,
                   last_cols)
    valid = rv_ref[i]

    def tile(width):
        def compute(rs):
            x = x_ref[rs, :]
            if norm:
                x = _adaln(x, gn_ref[...], sc_ref[...], sh_ref[...]).astype(BF16)
            g = _dot(x, wb_ref[0, :, :width])
            u = _dot(x, wb_ref[1, :, :width])
            o_ref[rs, :width] = (_silu(g) * u).astype(o_ref.dtype)

        def zero(rs):
            o_ref[rs, :width] = jnp.zeros((rs.stop - rs.start, width), o_ref.dtype)

        _by_valid_rows(valid, x_ref.shape[0], compute, zero)

    _by_block(j, pl.num_programs(0), o_ref.shape[1], last_cols, tile)


def _grouped_gateup(xs, wg, wu, tile_expert, n_tiles, next_group, rows_valid, tm, gain, mod, norm):
    p, d = xs.shape
    f = wg.shape[2]
    assert d % CAST_ROWS == 0 and tm % ROW_CHUNK == 0
    tf = _pick_cols(f, d, must_divide=False)
    n_blocks = pl.cdiv(f, tf)
    last_cols = f - (n_blocks - 1) * tf
    row = lambda part: pl.BlockSpec(
        (None, 1, d), lambda j, i, te, nt, nx, rv: (mod.row(part, i * tm if norm else 0), 0, 0))
    return pl.pallas_call(
        functools.partial(_gateup_kernel, norm=norm, last_cols=last_cols),
        grid_spec=pltpu.PrefetchScalarGridSpec(
            num_scalar_prefetch=4,
            grid=(n_blocks, p // tm),
            in_specs=[
                pl.BlockSpec((tm, d), lambda j, i, te, nt, nx, rv: (jnp.minimum(i, nt[0] - 1), 0)),
                pl.BlockSpec((1, d), lambda j, i, te, nt, nx, rv: (0, 0)),
                row(4), row(3),
                pl.BlockSpec(memory_space=pltpu.HBM),
                pl.BlockSpec(memory_space=pltpu.HBM),
            ],
            out_specs=pl.BlockSpec((tm, tf), lambda j, i, te, nt, nx, rv: (i, j)),
            scratch_shapes=[pltpu.VMEM((2, d, tf), F32), pltpu.VMEM((2, d, tf), BF16),
                            pltpu.SemaphoreType.DMA((2,))],
        ),
        out_shape=jax.ShapeDtypeStruct((p, f), BF16),
        compiler_params=_params(
            2 * d * tf * 6 + 2 * tm * d * xs.dtype.itemsize + 2 * tm * tf * 2
            + 4 * ROW_CHUNK * (tf + d) * 4, 2),
        name="grouped_gateup",
    )(tile_expert, n_tiles, next_group, rows_valid, xs, gain.reshape(1, d), mod.table, mod.table, wg, wu)


def _pack_pair(lo, hi):
    bl = lax.bitcast_convert_type(lo.astype(BF16).astype(F32), U32) >> 16
    bh = lax.bitcast_convert_type(hi.astype(BF16).astype(F32), U32) & jnp.uint32(0xFFFF0000)
    return bh | bl


def _unpack_pair(w):
    lo = lax.bitcast_convert_type(w << 16, F32)
    hi = lax.bitcast_convert_type(w & jnp.uint32(0xFFFF0000), F32)
    return lo, hi


def _down_kernel(te_ref, nt_ref, nxt_ref, rv_ref, a_ref, w_hbm, o_ref, stage_ref, wb_ref, sem, *, half_blocks):
    i = pl.program_id(1)
    slabs = [(w_hbm, lambda jj: jj), (w_hbm, lambda jj: half_blocks + jj)]
    _stage_weights(te_ref, nt_ref, nxt_ref, slabs, stage_ref, wb_ref, sem, stage_ref.shape[2])
    def compute(rs):
        a = a_ref[rs, :]
        o_ref[rs, :] = _pack_pair(_dot(a, wb_ref[0]), _dot(a, wb_ref[1]))

    def zero(rs):
        o_ref[rs, :] = jnp.zeros((rs.stop - rs.start, o_ref.shape[1]), o_ref.dtype)

    _by_valid_rows(rv_ref[i], a_ref.shape[0], compute, zero)


def _grouped_down(act, wd, tile_expert, n_tiles, next_group, rows_valid, tm):
    p, f = act.shape
    n = wd.shape[2]
    assert f % CAST_ROWS == 0 and tm % ROW_CHUNK == 0
    tn = _pick_cols(n // 2, f, must_divide=True)
    half_blocks = (n // 2) // tn
    return pl.pallas_call(
        functools.partial(_down_kernel, half_blocks=half_blocks),
        grid_spec=pltpu.PrefetchScalarGridSpec(
            num_scalar_prefetch=4,
            grid=(half_blocks, p // tm),
            in_specs=[
                pl.BlockSpec((tm, f), lambda j, i, te, nt, nx, rv: (jnp.minimum(i, nt[0] - 1), 0)),
                pl.BlockSpec(memory_space=pltpu.HBM),
            ],
            out_specs=pl.BlockSpec((tm, tn), lambda j, i, te, nt, nx, rv: (i, j)),
            scratch_shapes=[pltpu.VMEM((2, f, tn), F32), pltpu.VMEM((2, f, tn), BF16),
                            pltpu.SemaphoreType.DMA((2,))],
        ),
        out_shape=jax.ShapeDtypeStruct((p, n // 2), U32),
        compiler_params=_params(2 * f * tn * 6 + 2 * tm * f * 2 + 2 * tm * tn * 4 + 8 * ROW_CHUNK * tn * 4, 2),
        name="grouped_down",
    )(tile_expert, n_tiles, next_group, rows_valid, act, wd)


def _dense_ffn(x, gain, wg, wu, wd, widx, mod, tm=512):
    t = x.shape[0]
    nt = t // tm
    te = jnp.full((nt,), widx, I32)
    single_group = jnp.full((nt,), nt, I32)
    all_rows = jnp.full((nt,), tm, I32)
    act = _grouped_gateup(x, wg, wu, te, jnp.full((1,), nt, I32), single_group, all_rows, tm,
                          gain, mod, norm=True)
    return _proj_residual(act, wd, widx, x, mod, 5, tm=tm, tn=512)


def _router_kernel(x_ref, gn_ref, sc_ref, sh_ref, r_ref, h_ref, info_ref, cnt_ref, run_ref, *, n_exp):
    i = pl.program_id(0)
    tm = x_ref.shape[0]

    @pl.when(i == 0)
    def _():
        run_ref[...] = jnp.zeros(run_ref.shape, F32)

    h = _adaln(x_ref[...], gn_ref[...], sc_ref[...], sh_ref[...])
    half = h.shape[1] // 2
    h_ref[...] = _pack_pair(h[:, :half], h[:, half:])
    h_hi = h.astype(BF16)
    h_lo = (h - h_hi.astype(F32)).astype(BF16)
    r = r_ref[...]
    r_hi = r.astype(BF16)
    r_lo = (r - r_hi.astype(F32)).astype(BF16)
    logits = _dot(h_hi, r_hi) + (_dot(h_lo, r_hi) + _dot(h_hi, r_lo))

    lane = lax.broadcasted_iota(I32, (tm, LANES), 1).astype(F32)
    lg = jnp.where(lane < n_exp, logits, -jnp.inf)
    v1 = jnp.max(lg, axis=1, keepdims=True)
    i1 = jnp.min(jnp.where(lg == v1, lane, float(LANES)), axis=1, keepdims=True)
    lg2 = jnp.where(lane == i1, -jnp.inf, lg)
    v2 = jnp.max(lg2, axis=1, keepdims=True)
    i2 = jnp.min(jnp.where(lg2 == v2, lane, float(LANES)), axis=1, keepdims=True)
    e = jnp.exp(v2 - v1)
    g1 = 1.0 / (1.0 + e)
    g2 = e / (1.0 + e)

    oh = jnp.where((lane == i1) | (lane == i2), 1.0, 0.0)
    tri = jnp.where(lax.broadcasted_iota(I32, (tm, tm), 0) > lax.broadcasted_iota(I32, (tm, tm), 1),
                    1.0, 0.0).astype(BF16)
    cum = _dot(tri, oh.astype(BF16)) + run_ref[...]
    rank1 = jnp.sum(jnp.where(lane == i1, cum, 0.0), axis=1, keepdims=True)
    rank2 = jnp.sum(jnp.where(lane == i2, cum, 0.0), axis=1, keepdims=True)
    run = run_ref[...] + jnp.sum(oh, axis=0, keepdims=True)
    run_ref[...] = run
    cnt_ref[...] = jnp.broadcast_to(run, cnt_ref.shape)

    info = jnp.zeros((tm, LANES), F32)
    for k, val in enumerate((i1, i2, g1, g2, rank1, rank2)):
        info = jnp.where(lane == float(k), val, info)
    info_ref[...] = info


def _route(x, gain, mod, router, tm=512):
    t, d = x.shape
    n_exp = router.shape[1]
    assert n_exp <= LANES
    rpad = jnp.pad(router, ((0, 0), (0, LANES - n_exp)))
    row = lambda p: pl.BlockSpec((None, 1, d), lambda i: (mod.row(p, i * tm), 0, 0))
    return pl.pallas_call(
        functools.partial(_router_kernel, n_exp=n_exp),
        grid=(t // tm,),
        in_specs=[
            pl.BlockSpec((tm, d), lambda i: (i, 0)),
            pl.BlockSpec((1, d), lambda i: (0, 0)),
            row(4), row(3),
            pl.BlockSpec((d, LANES), lambda i: (0, 0)),
        ],
        out_specs=[
            pl.BlockSpec((tm, d // 2), lambda i: (i, 0)),
            pl.BlockSpec((tm, LANES), lambda i: (i, 0)),
            pl.BlockSpec((8, LANES), lambda i: (0, 0)),
        ],
        out_shape=[
            jax.ShapeDtypeStruct((t, d // 2), U32),
            jax.ShapeDtypeStruct((t, LANES), F32),
            jax.ShapeDtypeStruct((8, LANES), F32),
        ],
        scratch_shapes=[pltpu.VMEM((1, LANES), F32)],
        compiler_params=_params(4 * tm * d * 4 + 4 * tm * d * 4 + 2 * d * LANES * 4, 1),
        name="moe_route",
    )(x, gain.reshape(1, d), mod.table, mod.table, rpad)


ROW_DMA_UNROLL = 8


def _gather_kernel(tok_ref, tot_ref, hp_ref, o_ref, buf_ref):
    i = pl.program_id(0)
    tg, half = buf_ref.shape
    base = i * tg

    @pl.when(base < tot_ref[0])
    def _():
        def copy_row(r, c):
            buf_ref[pl.ds(r, 1), :] = hp_ref[pl.ds(tok_ref[base + r], 1), :]
            return c

        lax.fori_loop(0, tg, copy_row, 0, unroll=ROW_DMA_UNROLL)
        lo, hi = _unpack_pair(buf_ref[...])
        o_ref[:, :half] = lo.astype(o_ref.dtype)
        o_ref[:, half:] = hi.astype(o_ref.dtype)

    @pl.when(base >= tot_ref[0])
    def _():
        o_ref[...] = jnp.zeros(o_ref.shape, o_ref.dtype)


def _gather_rows(hp, sorted_tok, total_rows, tg=512):
    t, half = hp.shape
    p = sorted_tok.shape[0]
    return pl.pallas_call(
        _gather_kernel,
        grid_spec=pltpu.PrefetchScalarGridSpec(
            num_scalar_prefetch=2,
            grid=(p // tg,),
            in_specs=[pl.BlockSpec(memory_space=pltpu.VMEM)],
            out_specs=pl.BlockSpec((tg, 2 * half), lambda i, tok, tot: (i, 0)),
            scratch_shapes=[pltpu.VMEM((tg, half), U32)],
        ),
        out_shape=jax.ShapeDtypeStruct((p, 2 * half), BF16),
        compiler_params=_params(t * half * 4 + 8 * tg * half * 4, 1),
        name="moe_gather",
    )(sorted_tok, total_rows, hp)


def _combine_kernel(*refs, final):
    if final:
        pos_ref, y_hbm, x_ref, info_ref, gf_ref, nf_ref, o_ref, buf_ref, sem = refs
    else:
        pos_ref, y_hbm, x_ref, info_ref, gf_ref, o_ref, buf_ref, sem = refs
    i = pl.program_id(0)
    tc, half = buf_ref.shape[2], buf_ref.shape[3]

    def row_copy(slot, k, r, src_row):
        return pltpu.make_async_copy(
            y_hbm.at[pl.ds(src_row, 1)], buf_ref.at[slot, k, pl.ds(r, 1)], sem.at[slot])

    def fetch(tile):
        base = tile * tc

        def start(r, c):
            for k in range(MOE_TOPK):
                row_copy(tile % 2, k, r, pos_ref[MOE_TOPK * (base + r) + k]).start()
            return c

        lax.fori_loop(0, tc, start, 0, unroll=ROW_DMA_UNROLL)

    @pl.when(i == 0)
    def _():
        fetch(i)

    @pl.when(i + 1 < pl.num_programs(0))
    def _():
        fetch(i + 1)

    def wait(r, c):
        for k in range(MOE_TOPK):
            row_copy(i % 2, k, r, 0).wait()
        return c

    lax.fori_loop(0, tc, wait, 0, unroll=ROW_DMA_UNROLL)
    info = info_ref[...]
    y_lo = jnp.zeros((tc, half), F32)
    y_hi = jnp.zeros((tc, half), F32)
    for k in range(MOE_TOPK):
        lo, hi = _unpack_pair(buf_ref[i % 2, k])
        gate = info[:, MOE_TOPK + k:MOE_TOPK + k + 1]
        y_lo = y_lo + gate * lo
        y_hi = y_hi + gate * hi
    xn_lo = x_ref[:, :half] + gf_ref[:, :half] * y_lo
    xn_hi = x_ref[:, half:] + gf_ref[:, half:] * y_hi
    if final:
        ssq = jnp.sum(xn_lo * xn_lo, axis=-1, keepdims=True) + jnp.sum(xn_hi * xn_hi, axis=-1, keepdims=True)
        inv = lax.rsqrt(ssq / (2 * half) + NORM_EPS)
        xn_lo = xn_lo * inv * nf_ref[:, :half]
        xn_hi = xn_hi * inv * nf_ref[:, half:]
    o_ref[:, :half] = xn_lo
    o_ref[:, half:] = xn_hi


def _combine(y, pos_flat, x, info, mod, final_gain=None, tc=512):
    t, d = x.shape
    final = final_gain is not None
    in_specs = [
        pl.BlockSpec(memory_space=pl.ANY),
        pl.BlockSpec((tc, d), lambda i, pos: (i, 0)),
        pl.BlockSpec((tc, LANES), lambda i, pos: (i, 0)),
        pl.BlockSpec((None, 1, d), lambda i, pos: (mod.row(5, i * tc), 0, 0)),
    ]
    args = [pos_flat, y, x, info, mod.table]
    if final:
        in_specs.append(pl.BlockSpec((1, d), lambda i, pos: (0, 0)))
        args.append(final_gain.reshape(1, d))
    return pl.pallas_call(
        functools.partial(_combine_kernel, final=final),
        grid_spec=pltpu.PrefetchScalarGridSpec(
            num_scalar_prefetch=1,
            grid=(t // tc,),
            in_specs=in_specs,
            out_specs=pl.BlockSpec((tc, d), lambda i, pos: (i, 0)),
            scratch_shapes=[pltpu.VMEM((2, MOE_TOPK, tc, d // 2), U32),
                            pltpu.SemaphoreType.DMA((2,))],
        ),
        out_shape=jax.ShapeDtypeStruct((t, d), F32),
        compiler_params=_params(2 * tc * d * 4 + 4 * tc * d * 4 + 4 * tc * d * 4, 1),
        name="moe_combine",
    )(*args)


def _moe_ffn(x, gain, mod, router, wg, wu, wd, widx, final_gain=None, tm=512):
    t, d = x.shape
    n_exp = router.shape[1]
    wg, wu, wd = (w.reshape((-1,) + w.shape[2:]) for w in (wg, wu, wd))
    hf, info, cnt = _route(x, gain, mod, router)

    idx = info[:, 0:2].astype(I32)
    rank = info[:, 4:6].astype(I32)
    counts = cnt[0, :n_exp].astype(I32)
    padded = ((counts + tm - 1) // tm) * tm
    ends = jnp.cumsum(padded)
    starts = ends - padded
    pos = starts[idx] + rank
    p_max = MOE_TOPK * t + n_exp * tm
    tok = jnp.repeat(jnp.arange(t, dtype=I32), MOE_TOPK)
    sorted_tok = jnp.zeros((p_max,), I32).at[pos.reshape(-1)].set(
        tok, unique_indices=True, mode="promise_in_bounds")
    total = ends[-1:].astype(I32)
    n_tiles = total // tm
    tile_row = jnp.minimum(jnp.arange(p_max // tm, dtype=I32), n_tiles[0] - 1) * tm
    local_expert = jnp.minimum(
        jnp.sum((ends[None, :] <= tile_row[:, None]).astype(I32), axis=1), n_exp - 1)
    tile_expert = widx * n_exp + local_expert
    next_group = (ends // tm)[local_expert]
    tile_ids = jnp.arange(p_max // tm, dtype=I32)
    rows_valid = jnp.where(
        tile_ids < n_tiles[0],
        jnp.clip((starts + counts)[local_expert] - tile_ids * tm, 0, tm), 0).astype(I32)

    xs = _gather_rows(hf, sorted_tok, total)
    act = _grouped_gateup(xs, wg, wu, tile_expert, n_tiles, next_group, rows_valid, tm,
                          gain, mod, norm=False)
    y = _grouped_down(act, wd, tile_expert, n_tiles, next_group, rows_valid, tm)
    return _combine(y, pos.reshape(-1), x, info, mod, final_gain)


def kernel(x, c, mod_w, mod_b, norm_mix, norm_ffn, norm_final, moba_wqkv, moba_wo, sc_w_in, sc_conv,
           sc_w_out, pool_w, pool_scale, cf_w1, cf_b1, cf_dw, cf_dw_b, cf_ln_g, cf_ln_b, cf_w2, cf_b2,
           ffn_wg, ffn_wu, ffn_wd, moe_router, moe_wg, moe_wu, moe_wd):
    batch, seq, d = x.shape
    depth = mod_w.shape[0]
    table = _modulation(c, mod_w, mod_b)
    xt = x.reshape(batch * seq, d)
    for layer in range(depth):
        mod = _Mod(table, layer, batch, seq)
        m, j = layer % 4, layer // 4
        if m == 0:
            qkv = _qkv_rope(xt, norm_mix[layer], mod, moba_wqkv[j], seq)
            o = _moba_attention(qkv, batch, seq, d)
            xt = _proj_residual(o, moba_wo, j, xt, mod, 2)
        elif m == 1:
            h = _prenorm(xt, norm_mix[layer], mod, 0)
            v = _short_conv_in(h, sc_w_in[j], sc_conv[j], seq)
            xt = _proj_residual(v, sc_w_out, j, xt, mod, 2)
        elif m == 2:
            xt = _pool_mixer(xt, norm_mix[layer], mod, pool_w[j], pool_scale[j], seq)
        else:
            u = _conformer_glu(xt, norm_mix[layer], mod, cf_w1[j], cf_b1[j])
            z = _conformer_conv(u, cf_dw[j], cf_dw_b[j], cf_ln_g[j], cf_ln_b[j], seq)
            xt = _proj_residual(z, cf_w2, j, xt, mod, 2, bias=cf_b2[j])

        i = layer // 2
        if layer % 2 == 0:
            xt = _dense_ffn(xt, norm_ffn[layer], ffn_wg, ffn_wu, ffn_wd, i, mod)
        else:
            last = layer == depth - 1
            xt = _moe_ffn(xt, norm_ffn[layer], mod, moe_router[i], moe_wg, moe_wu, moe_wd, i,
                          final_gain=norm_final if last else None)
    if depth % 2 == 1:
        xt = _final_norm(xt, norm_final)
    return xt.reshape(batch, seq, d)
```
